```python
import math
import jax, jax.numpy as jnp
from jax import lax
import numpy as np

D_MODEL = 1024
BATCH = 32
SEQ = 256
DEPTH = 2
DEC_BATCH = 8
DEC_SEQ = 2048
PAST_LEN = 256

GRID_W = 64
N_HEADS = 8
HEAD_DIM = 64
V_DIM = 2 * HEAD_DIM
AXIS_DIM = HEAD_DIM // 2
ROPE_THETA = 10000.0
N_FOURIER_GROUPS = 4
FOURIER_GROUP = D_MODEL // N_FOURIER_GROUPS
D_FF = 2816
N_EXPERTS = 8
TOP_K = 2
Q_BLOCK = 128
EPS = 1e-6
SUBLN_EPS = 1e-5
N_MOD = 6
LAMBDA_INIT_0 = 0.8 - 0.6 * math.exp(-0.3 * 0)

kernel_name = "hybrid_diffattn_fnet_prefix_diffusion_step"

F32 = jnp.float32


def rmsnorm(x, g, eps=EPS):
    xf = x.astype(F32)
    y = xf * lax.rsqrt(jnp.mean(xf * xf, axis=-1, keepdims=True) + eps)
    return (y * g.astype(F32)).astype(x.dtype)


def adaln(cond, w, b, dtype):
    m = jax.nn.silu(cond.astype(F32)) @ w.astype(F32) + b.astype(F32)
    m = m.reshape(cond.shape[0], 1, N_MOD, D_MODEL).astype(dtype)
    return tuple(m[:, :, i] for i in range(N_MOD))


def modulate(h, shift, scale):
    return h * (1.0 + scale) + shift


def axial_rope_tables(length):
    rows = length // GRID_W
    row = jnp.repeat(jnp.arange(rows), GRID_W).astype(F32)
    col = jnp.tile(jnp.arange(GRID_W), rows).astype(F32)
    inv = 1.0 / (ROPE_THETA ** (jnp.arange(0, AXIS_DIM, 2, dtype=F32) / AXIS_DIM))
    ar = row[:, None] * inv[None, :]
    ac = col[:, None] * inv[None, :]
    ang = jnp.concatenate([ar, ar, ac, ac], axis=-1)
    return jnp.cos(ang), jnp.sin(ang)


def apply_axial_rope(x, cos, sin):
    xf = x.astype(F32)
    hf = AXIS_DIM // 2
    rot = jnp.concatenate([-xf[..., hf:AXIS_DIM], xf[..., :hf],
                           -xf[..., AXIS_DIM + hf:], xf[..., AXIS_DIM:AXIS_DIM + hf]], axis=-1)
    c = cos[:, None, None, :]
    s = sin[:, None, None, :]
    return (xf * c + rot * s).astype(x.dtype)


def diff_lambda(lq1, lk1, lq2, lk2, lambda_init):
    return (jnp.exp(jnp.sum(lq1.astype(F32) * lk1.astype(F32)))
            - jnp.exp(jnp.sum(lq2.astype(F32) * lk2.astype(F32))) + lambda_init)


def diff_project(h, w_qkv):
    b, l, _ = h.shape
    q, k, v = jnp.split(h @ w_qkv, 3, axis=-1)
    return (q.reshape(b, l, N_HEADS, 2, HEAD_DIM),
            k.reshape(b, l, N_HEADS, 2, HEAD_DIM),
            v.reshape(b, l, N_HEADS, V_DIM))


def diff_attend(q, k, v, lam):
    b, lq = q.shape[:2]
    nb = lq // Q_BLOCK
    qb = q.reshape(b, nb, Q_BLOCK, N_HEADS, 2, HEAD_DIM).transpose(1, 0, 2, 3, 4, 5)
    vf = v.astype(F32)
    scale = HEAD_DIM ** -0.5

    def one_block(qblk):
        s = jnp.einsum('bqhmd,bkhmd->bhmqk', qblk, k, preferred_element_type=F32) * scale
        p = jax.nn.softmax(s, axis=-1)
        a = p[:, :, 0] - lam * p[:, :, 1]
        return jnp.einsum('bhqk,bkhe->bqhe', a, vf)

    o = lax.map(one_block, qb)
    return o.transpose(1, 0, 2, 3, 4).reshape(b, lq, N_HEADS, V_DIM).astype(q.dtype)


def diff_out(o, subln_g, w_o, lambda_init):
    b, l = o.shape[:2]
    o = rmsnorm(o, subln_g, SUBLN_EPS) * (1.0 - lambda_init)
    return o.reshape(b, l, N_HEADS * V_DIM) @ w_o


def fourier_mix(h, w_f):
    b, l, d = h.shape
    hg = h.astype(F32).reshape(b, l, N_FOURIER_GROUPS, FOURIER_GROUP)
    f = jnp.fft.fft2(hg, axes=(1, 3), norm='ortho').real
    return f.reshape(b, l, d).astype(h.dtype) @ w_f


def swiglu(h, w_gu, w_down):
    g, u = jnp.split(h @ w_gu, 2, axis=-1)
    return (jax.nn.silu(g) * u) @ w_down


def moe_swiglu(h, w_router, w_gu_e, w_down_e):
    shp = h.shape
    hf = h.reshape(-1, D_MODEL)
    probs = jax.nn.softmax((hf @ w_router).astype(F32), axis=-1)
    topv, topi = lax.top_k(probs, TOP_K)
    topv = topv / jnp.sum(topv, axis=-1, keepdims=True)
    gates = jnp.sum(jax.nn.one_hot(topi, N_EXPERTS, dtype=F32) * topv[..., None], axis=1)
    out = jnp.zeros(hf.shape, F32)
    for e in range(N_EXPERTS):
        out = out + gates[:, e:e + 1] * swiglu(hf, w_gu_e[e], w_down_e[e]).astype(F32)
    return out.reshape(shp).astype(h.dtype)


def setup_inputs(seed: int = 0) -> dict:
    key = jax.random.key(seed)
    ks = iter(jax.random.split(key, 40))
    nrm = lambda shape, s: jax.random.normal(next(ks), shape, F32) * s
    gain = lambda n: 1.0 + 0.02 * jax.random.normal(next(ks), (n,), F32)
    d = D_MODEL
    inv_d = d ** -0.5
    return {
        "x_prompt": nrm((BATCH, SEQ, d), 1.0),
        "x_sample": nrm((DEC_BATCH, DEC_SEQ, d), 1.0),
        "c": nrm((DEC_BATCH, d), 1.0),
        "cache_k_0": nrm((DEC_BATCH, PAST_LEN, N_HEADS, 2 * HEAD_DIM), 1.0),
        "cache_v_0": nrm((DEC_BATCH, PAST_LEN, N_HEADS, V_DIM), 1.0),
        "c_ctx": nrm((d,), 1.0),
        "ada_w_0": nrm((d, N_MOD * d), 0.5 * inv_d),
        "ada_b_0": nrm((N_MOD * d,), 0.02),
        "norm1_g_0": gain(d),
        "norm2_g_0": gain(d),
        "w_qkv_0": nrm((d, 3 * N_HEADS * 2 * HEAD_DIM), inv_d),
        "lambda_q1_0": nrm((HEAD_DIM,), 0.1),
        "lambda_k1_0": nrm((HEAD_DIM,), 0.1),
        "lambda_q2_0": nrm((HEAD_DIM,), 0.1),
        "lambda_k2_0": nrm((HEAD_DIM,), 0.1),
        "subln_g_0": gain(V_DIM),
        "w_o_0": nrm((N_HEADS * V_DIM, d), inv_d),
        "w_gu_0": nrm((d, 2 * D_FF), inv_d),
        "w_down_0": nrm((D_FF, d), D_FF ** -0.5),
        "ada_w_1": nrm((d, N_MOD * d), 0.5 * inv_d),
        "ada_b_1": nrm((N_MOD * d,), 0.02),
        "norm1_g_1": gain(d),
        "norm2_g_1": gain(d),
        "w_fourier_1": nrm((d, d), inv_d),
        "w_router_1": nrm((d, N_EXPERTS), inv_d),
        "w_gu_e_1": nrm((N_EXPERTS, d, 2 * D_FF), inv_d),
        "w_down_e_1": nrm((N_EXPERTS, D_FF, d), D_FF ** -0.5),
        "final_norm_g": gain(d),
    }


def reference(x_prompt, x_sample, c, cache_k_0, cache_v_0, c_ctx,
              ada_w_0, ada_b_0, norm1_g_0, norm2_g_0, w_qkv_0,
              lambda_q1_0, lambda_k1_0, lambda_q2_0, lambda_k2_0, subln_g_0, w_o_0,
              w_gu_0, w_down_0,
              ada_w_1, ada_b_1, norm1_g_1, norm2_g_1, w_fourier_1,
              w_router_1, w_gu_e_1, w_down_e_1,
              final_norm_g):
    lam0 = diff_lambda(lambda_q1_0, lambda_k1_0, lambda_q2_0, lambda_k2_0, LAMBDA_INIT_0)

    x = x_prompt
    bp, lp, _ = x.shape
    s1, sc1, g1, s2, sc2, g2 = adaln(c_ctx[None, :], ada_w_0, ada_b_0, x.dtype)
    h = modulate(rmsnorm(x, norm1_g_0), s1, sc1)
    q, k, v = diff_project(h, w_qkv_0)
    o = diff_attend(q, k, v, lam0)
    x = x + g1 * diff_out(o, subln_g_0, w_o_0, LAMBDA_INIT_0)
    k_ctx_0 = k.reshape(bp, lp, N_HEADS, 2 * HEAD_DIM)
    v_ctx_0 = v
    x = x + g2 * swiglu(modulate(rmsnorm(x, norm2_g_0), s2, sc2), w_gu_0, w_down_0)

    s1, sc1, g1, s2, sc2, g2 = adaln(c_ctx[None, :], ada_w_1, ada_b_1, x.dtype)
    x = x + g1 * fourier_mix(modulate(rmsnorm(x, norm1_g_1), s1, sc1), w_fourier_1)
    x = x + g2 * moe_swiglu(modulate(rmsnorm(x, norm2_g_1), s2, sc2), w_router_1, w_gu_e_1, w_down_e_1)
    y_prompt = rmsnorm(x, final_norm_g)

    x = x_sample
    bs, ls, _ = x.shape
    cos, sin = axial_rope_tables(ls)
    s1, sc1, g1, s2, sc2, g2 = adaln(c, ada_w_0, ada_b_0, x.dtype)
    h = modulate(rmsnorm(x, norm1_g_0), s1, sc1)
    q, k, v = diff_project(h, w_qkv_0)
    q = apply_axial_rope(q, cos, sin)
    k = apply_axial_rope(k, cos, sin)
    pl = cache_k_0.shape[1]
    k_all = jnp.concatenate([k, cache_k_0.reshape(bs, pl, N_HEADS, 2, HEAD_DIM).astype(k.dtype)], axis=1)
    v_all = jnp.concatenate([v, cache_v_0.astype(v.dtype)], axis=1)
    o = diff_attend(q, k_all, v_all, lam0)
    x = x + g1 * diff_out(o, subln_g_0, w_o_0, LAMBDA_INIT_0)
    x = x + g2 * swiglu(modulate(rmsnorm(x, norm2_g_0), s2, sc2), w_gu_0, w_down_0)

    s1, sc1, g1, s2, sc2, g2 = adaln(c, ada_w_1, ada_b_1, x.dtype)
    x = x + g1 * fourier_mix(modulate(rmsnorm(x, norm1_g_1), s1, sc1), w_fourier_1)
    x = x + g2 * moe_swiglu(modulate(rmsnorm(x, norm2_g_1), s2, sc2), w_router_1, w_gu_e_1, w_down_e_1)
    y_sample = rmsnorm(x, final_norm_g)

    return (y_prompt, y_sample, k_ctx_0, v_ctx_0)
```

```python
import functools
import math

import jax
import jax.numpy as jnp
import numpy as np
from jax import lax
from jax.experimental import pallas as pl
from jax.experimental.pallas import tpu as pltpu

F32 = jnp.float32
BF16 = jnp.bfloat16

D_MODEL = 1024
N_HEADS = 8
HEAD_DIM = 64
V_DIM = 2 * HEAD_DIM
GRID_W = 64
AXIS_DIM = HEAD_DIM // 2
ROPE_THETA = 10000.0
N_FOURIER_GROUPS = 4
FOURIER_GROUP = D_MODEL // N_FOURIER_GROUPS
D_FF = 2816
N_EXPERTS = 8
N_MOD = 6
EPS = 1e-6
SUBLN_EPS = 1e-5
LAMBDA_INIT_0 = 0.8 - 0.6 * math.exp(-0.3 * 0)

V7X_VMEM_BYTES = 64 * 1024 * 1024
VMEM_LIMIT = V7X_VMEM_BYTES - 8 * 1024 * 1024
BF16_SUBLANES = 16

ADA_ROWS = 16
ADA_TN = 1536
TOKEN_TILE = 256
Q_TILE = 256
MOE_BLOCK = 512
MOE_ROWS = 2 * MOE_BLOCK + 128
EXPERT_TILE = 256
SEG_BITS = (512, 256, 128, 64, 32, 16)

_NT = (((1,), (1,)), ((), ()))


def _dot(a, b):
    return jnp.dot(a, b, preferred_element_type=F32)


def _split_bf16(x):
    hi = x.astype(BF16)
    lo = (x - hi.astype(F32)).astype(BF16)
    return hi, lo


def _rms(x, g, eps):
    return x * lax.rsqrt(jnp.mean(x * x, axis=-1, keepdims=True) + eps) * g


def _cparams(n_grid, vmem=VMEM_LIMIT):
    return pltpu.CompilerParams(dimension_semantics=("arbitrary",) * n_grid, vmem_limit_bytes=vmem)


def _const_spec(shape):
    nd = len(shape)
    return pl.BlockSpec(shape, lambda *_: (0,) * nd, pipeline_mode=pl.Buffered(1))


def _adaln_kernel(c_ref, w_ref, b_ref, o_ref):
    c = c_ref[...]
    a_hi, a_lo = _split_bf16(c * jax.nn.sigmoid(c))
    w_hi, w_lo = _split_bf16(w_ref[...])
    o_ref[...] = _dot(a_hi, w_hi) + _dot(a_hi, w_lo) + _dot(a_lo, w_hi) + b_ref[...]


def _adaln(cond, w, b):
    n = N_MOD * D_MODEL
    out = pl.pallas_call(
        _adaln_kernel,
        grid=(n // ADA_TN,),
        in_specs=[pl.BlockSpec((ADA_ROWS, D_MODEL), lambda j: (0, 0)),
                  pl.BlockSpec((D_MODEL, ADA_TN), lambda j: (0, j)),
                  pl.BlockSpec((1, ADA_TN), lambda j: (0, j))],
        out_specs=pl.BlockSpec((ADA_ROWS, ADA_TN), lambda j: (0, j)),
        out_shape=jax.ShapeDtypeStruct((ADA_ROWS, n), F32),
        compiler_params=_cparams(1),
        name="adaln",
    )(cond, w, b.reshape(1, n))
    return out.reshape(ADA_ROWS, N_MOD, D_MODEL)


def _mod_spec(seq_tiles, row0):
    if row0 == 0:
        return pl.BlockSpec((1, N_MOD, D_MODEL), lambda i, *_: (0, 0, 0))
    return pl.BlockSpec((1, N_MOD, D_MODEL), lambda i, *_: (row0 + i // seq_tiles, 0, 0))


def _qkv_kernel(*refs, rope):
    if rope:
        x_ref, mod_ref, g_ref, w_ref, cos_ref, sa_ref, sb_ref, q_ref, k_ref, v_ref = refs
    else:
        x_ref, mod_ref, g_ref, w_ref, q_ref, k_ref, v_ref = refs
    h = _rms(x_ref[...], g_ref[...], EPS)
    h = h * (1.0 + mod_ref[0, 1:2, :]) + mod_ref[0, 0:1, :]
    qkv = _dot(h.astype(BF16), w_ref[...])
    inner = N_HEADS * 2 * HEAD_DIM
    for which, ref in ((0, q_ref), (1, k_ref), (2, v_ref)):
        part = qkv[:, which * inner:(which + 1) * inner]
        if rope and which < 2:
            cos, sa, sb = cos_ref[...], sa_ref[...], sb_ref[...]
            for hd in range(N_HEADS):
                blk = part[:, hd * V_DIM:(hd + 1) * V_DIM]
                blk = (blk * cos + pltpu.roll(blk, V_DIM - AXIS_DIM // 2, 1) * sa
                       + pltpu.roll(blk, AXIS_DIM // 2, 1) * sb)
                if which == 0:
                    blk = blk * (HEAD_DIM ** -0.5)
                ref[:, hd * V_DIM:(hd + 1) * V_DIM] = blk.astype(ref.dtype)
        else:
            if which == 0:
                part = part * (HEAD_DIM ** -0.5)
            ref[...] = part.astype(ref.dtype)


def _rope_tables(length):
    rows = length // GRID_W
    row = jnp.repeat(jnp.arange(rows), GRID_W).astype(F32)
    col = jnp.tile(jnp.arange(GRID_W), rows).astype(F32)
    inv = 1.0 / (ROPE_THETA ** (jnp.arange(0, AXIS_DIM, 2, dtype=F32) / AXIS_DIM))
    ar = row[:, None] * inv[None, :]
    ac = col[:, None] * inv[None, :]
    ang = jnp.concatenate([ar, ar, ac, ac], axis=-1)
    cos, sin = jnp.cos(ang), jnp.sin(ang)
    first = (jnp.arange(HEAD_DIM) % AXIS_DIM) < (AXIS_DIM // 2)
    sin_a = jnp.where(first[None, :], -sin, 0.0)
    sin_b = jnp.where(first[None, :], 0.0, sin)
    wide = lambda t: jnp.concatenate([t, t], axis=-1)
    return wide(cos), wide(sin_a), wide(sin_b)


def _qkv(x2d, mod, g, w_bf16, seq_len, row0, rope, kv_dtype):
    t = x2d.shape[0]
    tm = TOKEN_TILE
    seq_tiles = seq_len // tm
    row_spec = pl.BlockSpec((tm, D_MODEL), lambda i: (i, 0))
    in_specs = [row_spec, _mod_spec(seq_tiles, row0), _const_spec((1, D_MODEL)),
                _const_spec((D_MODEL, 3 * D_MODEL))]
    args = [x2d, mod, g.reshape(1, D_MODEL), w_bf16]
    if rope:
        tab_spec = pl.BlockSpec((tm, V_DIM), lambda i: (i % seq_tiles, 0))
        in_specs += [tab_spec] * 3
        args += list(_rope_tables(seq_len))
    return pl.pallas_call(
        functools.partial(_qkv_kernel, rope=rope),
        grid=(t // tm,),
        in_specs=in_specs,
        out_specs=[row_spec] * 3,
        out_shape=[jax.ShapeDtypeStruct((t, D_MODEL), BF16),
                   jax.ShapeDtypeStruct((t, D_MODEL), kv_dtype),
                   jax.ShapeDtypeStruct((t, D_MODEL), kv_dtype)],
        compiler_params=_cparams(1),
        name="qkv",
    )(*args)


def _attn_kernel(*refs, has_cache):
    if has_cache:
        lam_ref, sg_ref, q_ref, k_ref, v_ref, ck_ref, cv_ref, o_ref = refs
    else:
        lam_ref, sg_ref, q_ref, k_ref, v_ref, o_ref = refs
    lp = lam_ref[...]
    lam = (jnp.exp(jnp.sum(lp[0:1] * lp[1:2], axis=-1, keepdims=True))
           - jnp.exp(jnp.sum(lp[2:3] * lp[3:4], axis=-1, keepdims=True)) + LAMBDA_INIT_0)
    q = q_ref[...]
    lane = lax.broadcasted_iota(jnp.int32, (1, V_DIM), 1)
    zero = jnp.zeros_like(q)
    keys = [(k_ref[0].astype(BF16), v_ref[0].astype(BF16))]
    if has_cache:
        keys.append((ck_ref[0].astype(BF16), cv_ref[0].astype(BF16)))
    outs = []
    for comp in range(2):
        sel = (lane < HEAD_DIM) if comp == 0 else (lane >= HEAD_DIM)
        qc = jnp.where(sel, q, zero)
        s = [lax.dot_general(qc, k, _NT, preferred_element_type=F32) for k, _ in keys]
        m = functools.reduce(jnp.maximum, [jnp.max(x, axis=-1, keepdims=True) for x in s])
        p = [jnp.exp(x - m) for x in s]
        l = functools.reduce(jnp.add, [jnp.sum(x, axis=-1, keepdims=True) for x in p])
        o = functools.reduce(jnp.add, [_dot(x.astype(BF16), v) for x, (_, v) in zip(p, keys)])
        outs.append(o * (1.0 / l))
    o = outs[0] - lam * outs[1]
    o_ref[...] = (_rms(o, sg_ref[...], SUBLN_EPS) * (1.0 - LAMBDA_INIT_0)).astype(o_ref.dtype)


def _attention(lam_params, subln_g, q, k, v, batch, seq_len, cache_k=None, cache_v=None):
    tq = Q_TILE
    nq = seq_len // tq
    has_cache = cache_k is not None
    k3 = k.reshape(batch, seq_len, D_MODEL)
    v3 = v.reshape(batch, seq_len, D_MODEL)
    q_spec = pl.BlockSpec((tq, V_DIM), lambda b, h, i: (b * nq + i, h))
    kv_spec = pl.BlockSpec((1, seq_len, V_DIM), lambda b, h, i: (b, 0, h))
    in_specs = [_const_spec((4, HEAD_DIM)), _const_spec((1, V_DIM)), q_spec, kv_spec, kv_spec]
    args = [lam_params, subln_g.reshape(1, V_DIM), q, k3, v3]
    if has_cache:
        past = cache_k.shape[1]
        c_spec = pl.BlockSpec((1, past, V_DIM), lambda b, h, i: (b, 0, h))
        in_specs += [c_spec, c_spec]
        args += [cache_k.reshape(batch, past, D_MODEL), cache_v.reshape(batch, past, D_MODEL)]
    return pl.pallas_call(
        functools.partial(_attn_kernel, has_cache=has_cache),
        grid=(batch, N_HEADS, nq),
        in_specs=in_specs,
        out_specs=q_spec,
        out_shape=jax.ShapeDtypeStruct((batch * seq_len, D_MODEL), BF16),
        compiler_params=_cparams(3),
        name="diff_attn",
    )(*args)


def _silu_mul(gu):
    g = gu[:, :D_FF]
    return (g * jax.nn.sigmoid(g) * gu[:, D_FF:]).astype(BF16)


def _post_attn_kernel(x_ref, o_ref, mod_ref, g_ref, wo_ref, wgu_ref, wd_ref, out_ref):
    x1 = x_ref[...] + mod_ref[0, 2:3, :] * _dot(o_ref[...], wo_ref[...])
    h = _rms(x1, g_ref[...], EPS) * (1.0 + mod_ref[0, 4:5, :]) + mod_ref[0, 3:4, :]
    act = _silu_mul(_dot(h.astype(BF16), wgu_ref[...]))
    out_ref[...] = x1 + mod_ref[0, 5:6, :] * _dot(act, wd_ref[...])


def _post_attn(x2d, o, mod, g, wo, wgu, wd, seq_len, row0):
    t = x2d.shape[0]
    tm = TOKEN_TILE
    row_spec = pl.BlockSpec((tm, D_MODEL), lambda i: (i, 0))
    return pl.pallas_call(
        _post_attn_kernel,
        grid=(t // tm,),
        in_specs=[row_spec, row_spec, _mod_spec(seq_len // tm, row0), _const_spec((1, D_MODEL)),
                  _const_spec((D_MODEL, D_MODEL)), _const_spec((D_MODEL, 2 * D_FF)),
                  _const_spec((D_FF, D_MODEL))],
        out_specs=row_spec,
        out_shape=jax.ShapeDtypeStruct((t, D_MODEL), F32),
        compiler_params=_cparams(1),
        name="post_attn_swiglu",
    )(x2d, o, mod, g.reshape(1, D_MODEL), wo, wgu, wd)


def _dft_tables(n):
    j = np.arange(n, dtype=np.int64)
    ang = (2.0 * np.pi / n) * ((j[:, None] * j[None, :]) % n).astype(np.float64)
    s = n ** -0.5
    return np.cos(ang) * s, np.sin(ang) * s


def _fourier_ch_kernel(x_ref, mod_ref, g_ref, cs_ref, a_ref, b_ref):
    h = _rms(x_ref[...], g_ref[...], EPS) * (1.0 + mod_ref[0, 1:2, :]) + mod_ref[0, 0:1, :]
    hb = h.astype(BF16)
    fg = FOURIER_GROUP
    for grp in range(N_FOURIER_GROUPS):
        ab = _dot(hb[:, grp * fg:(grp + 1) * fg], cs_ref[...])
        a_ref[:, grp * fg:(grp + 1) * fg] = ab[:, :fg].astype(BF16)
        b_ref[:, grp * fg:(grp + 1) * fg] = ab[:, fg:].astype(BF16)


def _fourier_pos_kernel(x_ref, mod_ref, cl_ref, sl_ref, a_ref, b_ref, wf_ref, out_ref):
    y = _dot(cl_ref[...], a_ref[0]) + _dot(sl_ref[...], b_ref[0])
    out_ref[...] = x_ref[...] + mod_ref[0, 2:3, :] * _dot(y.astype(BF16), wf_ref[...])


def _fourier(x2d, mod, g, wf, batch, seq_len, row0):
    t = x2d.shape[0]
    tm = TOKEN_TILE
    cd, sd = _dft_tables(FOURIER_GROUP)
    cs = jnp.asarray(np.concatenate([cd, sd], axis=1), dtype=F32).astype(BF16)
    row_spec = pl.BlockSpec((tm, D_MODEL), lambda i: (i, 0))
    a, b = pl.pallas_call(
        _fourier_ch_kernel,
        grid=(t // tm,),
        in_specs=[row_spec, _mod_spec(seq_len // tm, row0), _const_spec((1, D_MODEL)),
                  _const_spec((FOURIER_GROUP, 2 * FOURIER_GROUP))],
        out_specs=[row_spec, row_spec],
        out_shape=[jax.ShapeDtypeStruct((t, D_MODEL), BF16)] * 2,
        compiler_params=_cparams(1),
        name="fourier_channels",
    )(x2d, mod, g.reshape(1, D_MODEL), cs)

    cl, sl = _dft_tables(seq_len)
    cl = jnp.asarray(cl, dtype=F32).astype(BF16)
    sl = jnp.asarray(-sl, dtype=F32).astype(BF16)
    tr = min(seq_len, 512)
    nr = seq_len // tr
    xr_spec = pl.BlockSpec((tr, D_MODEL), lambda bb, r: (bb * nr + r, 0))
    tab_spec = pl.BlockSpec((tr, seq_len), lambda bb, r: (r, 0))
    ab_spec = pl.BlockSpec((1, seq_len, D_MODEL), lambda bb, r: (bb, 0, 0))
    if row0 == 0:
        mod_spec = pl.BlockSpec((1, N_MOD, D_MODEL), lambda bb, r: (0, 0, 0))
    else:
        mod_spec = pl.BlockSpec((1, N_MOD, D_MODEL), lambda bb, r: (row0 + bb, 0, 0))
    return pl.pallas_call(
        _fourier_pos_kernel,
        grid=(batch, nr),
        in_specs=[xr_spec, mod_spec, tab_spec, tab_spec, ab_spec, ab_spec,
                  _const_spec((D_MODEL, D_MODEL))],
        out_specs=xr_spec,
        out_shape=jax.ShapeDtypeStruct((t, D_MODEL), F32),
        compiler_params=_cparams(2),
        name="fourier_positions",
    )(x2d, mod, cl, sl, a.reshape(batch, seq_len, D_MODEL), b.reshape(batch, seq_len, D_MODEL), wf)


def _router_kernel(x_ref, mod_ref, g_ref, wr_ref, h_ref, route_ref, cnt_ref):
    tb = MOE_BLOCK
    h = _rms(x_ref[...], g_ref[...], EPS) * (1.0 + mod_ref[0, 4:5, :]) + mod_ref[0, 3:4, :]
    h_hi, h_lo = _split_bf16(h)
    h_ref[...] = h_hi
    w_hi, w_lo = _split_bf16(wr_ref[...])
    dg = lambda a, b: lax.dot_general(a, b, _NT, preferred_element_type=F32)
    logits = dg(w_hi, h_hi) + dg(w_lo, h_hi) + dg(w_hi, h_lo)
    e = jnp.exp(logits - jnp.max(logits, axis=0, keepdims=True))
    probs = e / jnp.sum(e, axis=0, keepdims=True)
    eidx = lax.broadcasted_iota(jnp.int32, (N_EXPERTS, tb), 0).astype(F32)
    big = float(N_EXPERTS)
    p1 = jnp.max(probs, axis=0, keepdims=True)
    i1 = jnp.min(jnp.where(probs == p1, eidx, big), axis=0, keepdims=True)
    oh1 = eidx == i1
    rest = jnp.where(oh1, -1.0, probs)
    p2 = jnp.max(rest, axis=0, keepdims=True)
    i2 = jnp.min(jnp.where(rest == p2, eidx, big), axis=0, keepdims=True)
    oh2 = eidx == i2
    den = p1 + p2
    oh = jnp.where(oh1 | oh2, 1.0, 0.0)
    before = (lax.broadcasted_iota(jnp.int32, (tb, tb), 0)
              < lax.broadcasted_iota(jnp.int32, (tb, tb), 1))
    rank = _dot(oh.astype(BF16), jnp.where(before, 1.0, 0.0).astype(BF16))
    r1 = jnp.sum(jnp.where(oh1, rank, 0.0), axis=0, keepdims=True)
    r2 = jnp.sum(jnp.where(oh2, rank, 0.0), axis=0, keepdims=True)
    zero = jnp.zeros_like(r1)
    route_ref[...] = jnp.concatenate([i1, i2, p1 / den, p2 / den, r1, r2, zero, zero], axis=0)
    cnt = jnp.sum(oh, axis=1, keepdims=True)
    cnt_ref[0] = jnp.broadcast_to(cnt, (N_EXPERTS, 128))


def _router(x2d, mod, g, wr_t, seq_len, row0):
    t = x2d.shape[0]
    tb = MOE_BLOCK
    nb = t // tb
    row_spec = pl.BlockSpec((tb, D_MODEL), lambda i: (i, 0))
    return pl.pallas_call(
        _router_kernel,
        grid=(nb,),
        in_specs=[row_spec, _mod_spec(seq_len // tb, row0), _const_spec((1, D_MODEL)),
                  _const_spec((N_EXPERTS, D_MODEL))],
        out_specs=[row_spec, pl.BlockSpec((8, tb), lambda i: (0, i)),
                   pl.BlockSpec((1, N_EXPERTS, 128), lambda i: (i, 0, 0))],
        out_shape=[jax.ShapeDtypeStruct((t, D_MODEL), BF16),
                   jax.ShapeDtypeStruct((8, t), F32),
                   jax.ShapeDtypeStruct((nb, N_EXPERTS, 128), F32)],
        compiler_params=_cparams(1),
        name="router",
    )(x2d, mod, g.reshape(1, D_MODEL), wr_t)


def _segment_copies(n16, src_row, dst_row, make_copy):
    for bit in SEG_BITS:
        done = n16 & ~(2 * bit - 1)

        @pl.when((n16 & bit) != 0)
        def _():
            make_copy(pl.multiple_of(src_row + done, BF16_SUBLANES),
                      pl.multiple_of(dst_row + done, BF16_SUBLANES), bit)


def _segment_start(seg_ref, base, idx):
    out = jnp.zeros_like(idx)
    for e in range(N_EXPERTS):
        out = jnp.where(idx == float(e), seg_ref[base + e].astype(F32), out)
    return out


def _dispatch_kernel(seg_ref, dst_ref, n16_ref, h_ref, route_ref, xs_in_ref, xs_ref, comp_ref, sem,
                     *, block0):
    del xs_in_ref
    base = (block0 + pl.program_id(0)) * N_EXPERTS
    r = route_ref[...]
    pos1 = _segment_start(seg_ref, base, r[0:1]) + r[4:5]
    pos2 = _segment_start(seg_ref, base, r[1:2]) + r[5:6]
    rows = lax.broadcasted_iota(jnp.int32, (MOE_ROWS, MOE_BLOCK), 0).astype(F32)
    onehot = jnp.where((rows == pos1) | (rows == pos2), 1.0, 0.0).astype(BF16)
    comp_ref[...] = _dot(onehot, h_ref[...]).astype(BF16)

    def copy(src, dst, rows_):
        return pltpu.make_async_copy(comp_ref.at[pl.ds(src, rows_)], xs_ref.at[pl.ds(dst, rows_)], sem)

    for e in range(N_EXPERTS):
        _segment_copies(n16_ref[base + e], seg_ref[base + e], dst_ref[base + e],
                        lambda s, d, n: copy(s, d, n).start())
    for e in range(N_EXPERTS):
        _segment_copies(n16_ref[base + e], seg_ref[base + e], dst_ref[base + e],
                        lambda s, d, n: copy(s, d, n).wait())


def _dispatch(tables, h, route, xs, block0):
    t = h.shape[0]
    grid_spec = pltpu.PrefetchScalarGridSpec(
        num_scalar_prefetch=3,
        grid=(t // MOE_BLOCK,),
        in_specs=[pl.BlockSpec((MOE_BLOCK, D_MODEL), lambda i, *_: (i, 0)),
                  pl.BlockSpec((8, MOE_BLOCK), lambda i, *_: (0, i)),
                  pl.BlockSpec(memory_space=pl.ANY)],
        out_specs=pl.BlockSpec(memory_space=pl.ANY),
        scratch_shapes=[pltpu.VMEM((MOE_ROWS, D_MODEL), BF16), pltpu.SemaphoreType.DMA(())],
    )
    return pl.pallas_call(
        functools.partial(_dispatch_kernel, block0=block0),
        grid_spec=grid_spec,
        out_shape=jax.ShapeDtypeStruct(xs.shape, xs.dtype),
        input_output_aliases={5: 0},
        compiler_params=_cparams(1),
        name="moe_dispatch",
    )(*tables, h, route, xs)


def _expert_kernel(te_ref, nt_ref, xs_ref, wgu_ref, wd_ref, ys_ref):
    del te_ref
    i = pl.program_id(0)

    @pl.when(i < nt_ref[0])
    def _():
        act = _silu_mul(_dot(xs_ref[...], wgu_ref[0]))
        ys_ref[...] = _dot(act, wd_ref[0]).astype(ys_ref.dtype)

    @pl.when(i >= nt_ref[0])
    def _():
        ys_ref[...] = jnp.zeros_like(ys_ref)


def _experts(tile_expert, n_tiles, xs, wgu_e, wd_e):
    rows = xs.shape[0]
    tm = EXPERT_TILE
    row_spec = pl.BlockSpec((tm, D_MODEL), lambda i, te, nt: (i, 0))
    grid_spec = pltpu.PrefetchScalarGridSpec(
        num_scalar_prefetch=2,
        grid=(rows // tm,),
        in_specs=[row_spec,
                  pl.BlockSpec((1, D_MODEL, 2 * D_FF), lambda i, te, nt: (te[i], 0, 0)),
                  pl.BlockSpec((1, D_FF, D_MODEL), lambda i, te, nt: (te[i], 0, 0))],
        out_specs=row_spec,
    )
    return pl.pallas_call(
        _expert_kernel,
        grid_spec=grid_spec,
        out_shape=jax.ShapeDtypeStruct((rows, D_MODEL), BF16),
        compiler_params=_cparams(1),
        name="moe_experts",
    )(tile_expert, n_tiles, xs, wgu_e, wd_e)


def _combine_kernel(seg_ref, dst_ref, n16_ref, x_ref, rt_ref, mod_ref, g_ref, ys_ref, out_ref,
                    buf_ref, sem, *, block0):
    base = (block0 + pl.program_id(0)) * N_EXPERTS
    buf_ref[...] = jnp.zeros_like(buf_ref)

    def copy(src, dst, rows_):
        return pltpu.make_async_copy(ys_ref.at[pl.ds(dst, rows_)], buf_ref.at[pl.ds(src, rows_)], sem)

    for e in range(N_EXPERTS):
        _segment_copies(n16_ref[base + e], seg_ref[base + e], dst_ref[base + e],
                        lambda s, d, n: copy(s, d, n).start())
    rt = rt_ref[...]
    pos1 = _segment_start(seg_ref, base, rt[:, 0:1]) + rt[:, 4:5]
    pos2 = _segment_start(seg_ref, base, rt[:, 1:2]) + rt[:, 5:6]
    cols = lax.broadcasted_iota(jnp.int32, (MOE_BLOCK, MOE_ROWS), 1).astype(F32)
    gates = (jnp.where(cols == pos1, rt[:, 2:3], 0.0)
             + jnp.where(cols == pos2, rt[:, 3:4], 0.0)).astype(BF16)
    for e in range(N_EXPERTS):
        _segment_copies(n16_ref[base + e], seg_ref[base + e], dst_ref[base + e],
                        lambda s, d, n: copy(s, d, n).wait())
    x = x_ref[...] + mod_ref[0, 5:6, :] * _dot(gates, buf_ref[...])
    out_ref[...] = _rms(x, g_ref[...], EPS)


def _combine(tables, x2d, route_t, mod, g, ys, seq_len, row0, block0):
    t = x2d.shape[0]
    tb = MOE_BLOCK
    grid_spec = pltpu.PrefetchScalarGridSpec(
        num_scalar_prefetch=3,
        grid=(t // tb,),
        in_specs=[pl.BlockSpec((tb, D_MODEL), lambda i, *_: (i, 0)),
                  pl.BlockSpec((tb, 8), lambda i, *_: (i, 0)),
                  _mod_spec(seq_len // tb, row0),
                  pl.BlockSpec((1, D_MODEL), lambda i, *_: (0, 0)),
                  pl.BlockSpec(memory_space=pl.ANY)],
        out_specs=pl.BlockSpec((tb, D_MODEL), lambda i, *_: (i, 0)),
        scratch_shapes=[pltpu.VMEM((MOE_ROWS, D_MODEL), BF16), pltpu.SemaphoreType.DMA(())],
    )
    return pl.pallas_call(
        functools.partial(_combine_kernel, block0=block0),
        grid_spec=grid_spec,
        out_shape=jax.ShapeDtypeStruct((t, D_MODEL), F32),
        compiler_params=_cparams(1),
        name="moe_combine",
    )(*tables, x2d, route_t, mod, g.reshape(1, D_MODEL), ys)


def _moe_tables(counts, n_rows):
    pad = BF16_SUBLANES
    n16 = (counts + pad - 1) // pad * pad
    seg = jnp.cumsum(n16, axis=1) - n16
    total = jnp.sum(n16, axis=0)
    region = (total + EXPERT_TILE - 1) // EXPERT_TILE * EXPERT_TILE
    region_end = jnp.cumsum(region)
    dst = (region_end - region)[None, :] + jnp.cumsum(n16, axis=0) - n16
    tiles_end = region_end // EXPERT_TILE
    tile_ids = jnp.arange(n_rows // EXPERT_TILE, dtype=jnp.int32)
    tile_expert = jnp.minimum(jnp.sum(tile_ids[:, None] >= tiles_end[None, :], axis=1), N_EXPERTS - 1)
    flat = lambda a: a.reshape(-1).astype(jnp.int32)
    return (flat(seg), flat(dst), flat(n16)), tile_expert.astype(jnp.int32), tiles_end[-1:].astype(jnp.int32)


def kernel(x_prompt, x_sample, c, cache_k_0, cache_v_0, c_ctx, ada_w_0, ada_b_0, norm1_g_0, norm2_g_0, w_qkv_0, lambda_q1_0, lambda_k1_0, lambda_q2_0, lambda_k2_0, subln_g_0, w_o_0, w_gu_0, w_down_0, ada_w_1, ada_b_1, norm1_g_1, norm2_g_1, w_fourier_1, w_router_1, w_gu_e_1, w_down_e_1, final_norm_g):
    bp, lp, _ = x_prompt.shape
    bs, ls, _ = x_sample.shape
    assert 1 + bs <= ADA_ROWS and (bp * lp) % MOE_BLOCK == 0 and ls % MOE_BLOCK == 0

    cond = jnp.zeros((ADA_ROWS, D_MODEL), F32).at[0].set(c_ctx).at[1:1 + bs].set(c)
    mod0 = _adaln(cond, ada_w_0, ada_b_0)
    mod1 = _adaln(cond, ada_w_1, ada_b_1)
    lam_params = jnp.stack([lambda_q1_0, lambda_k1_0, lambda_q2_0, lambda_k2_0])

    w_qkv = w_qkv_0.astype(BF16)
    w_o = w_o_0.astype(BF16)
    w_gu = w_gu_0.astype(BF16)
    w_down = w_down_0.astype(BF16)
    w_f = w_fourier_1.astype(BF16)
    w_gu_e = w_gu_e_1.astype(BF16)
    w_down_e = w_down_e_1.astype(BF16)
    w_router_t = w_router_1.T

    groups = [dict(x=x_prompt.reshape(bp * lp, D_MODEL), batch=bp, seq=lp, row0=0, rope=False),
              dict(x=x_sample.reshape(bs * ls, D_MODEL), batch=bs, seq=ls, row0=1, rope=True)]

    k_ctx = v_ctx = None
    for gr in groups:
        x, batch, seq, row0 = gr["x"], gr["batch"], gr["seq"], gr["row0"]
        q, k, v = _qkv(x, mod0, norm1_g_0, w_qkv, seq, row0, gr["rope"], BF16 if gr["rope"] else F32)
        if gr["rope"]:
            o = _attention(lam_params, subln_g_0, q, k, v, batch, seq, cache_k_0, cache_v_0)
        else:
            k_ctx, v_ctx = k, v
            o = _attention(lam_params, subln_g_0, q, k, v, batch, seq)
        x = _post_attn(x, o, mod0, norm2_g_0, w_o, w_gu, w_down, seq, row0)
        x = _fourier(x, mod1, norm1_g_1, w_f, batch, seq, row0)
        gr["x"] = x
        gr["h"], gr["route"], gr["cnt"] = _router(x, mod1, norm2_g_1, w_router_t, seq, row0)

    n_blocks = [gr["x"].shape[0] // MOE_BLOCK for gr in groups]
    n_pairs = 2 * sum(gr["x"].shape[0] for gr in groups)
    max_rows = n_pairs + sum(n_blocks) * N_EXPERTS * (BF16_SUBLANES - 1) + N_EXPERTS * EXPERT_TILE
    max_rows = (max_rows + EXPERT_TILE - 1) // EXPERT_TILE * EXPERT_TILE
    counts = jnp.concatenate([gr["cnt"][:, :, 0] for gr in groups], axis=0).astype(jnp.int32)
    tables, tile_expert, n_tiles = _moe_tables(counts, max_rows)

    xs = jnp.zeros((max_rows, D_MODEL), BF16)
    block0 = 0
    for gr, nb in zip(groups, n_blocks):
        xs = _dispatch(tables, gr["h"], gr["route"], xs, block0)
        block0 += nb
    ys = _experts(tile_expert, n_tiles, xs, w_gu_e, w_down_e)
    outs = []
    block0 = 0
    for gr, nb in zip(groups, n_blocks):
        outs.append(_combine(tables, gr["x"], gr["route"].T, mod1, final_norm_g, ys,
                             gr["seq"], gr["row0"], block0))
        block0 += nb

    y_prompt = outs[0].reshape(bp, lp, D_MODEL)
    y_sample = outs[1].reshape(bs, ls, D_MODEL)
    return (y_prompt, y_sample,
            k_ctx.reshape(bp, lp, N_HEADS, 2 * HEAD_DIM), v_ctx.reshape(bp, lp, N_HEADS, V_DIM))
```

```python
import functools
import math

import jax
import jax.numpy as jnp
import numpy as np
from jax import lax
from jax.experimental import pallas as pl
from jax.experimental.pallas import tpu as pltpu

F32 = jnp.float32
BF16 = jnp.bfloat16

D_MODEL = 1024
N_HEADS = 8
HEAD_DIM = 64
V_DIM = 2 * HEAD_DIM
GRID_W = 64
AXIS_DIM = HEAD_DIM // 2
ROPE_THETA = 10000.0
N_FOURIER_GROUPS = 4
FOURIER_GROUP = D_MODEL // N_FOURIER_GROUPS
D_FF = 2816
N_EXPERTS = 8
N_MOD = 6
EPS = 1e-6
SUBLN_EPS = 1e-5
LAMBDA_INIT_0 = 0.8 - 0.6 * math.exp(-0.3 * 0)
Q_SCALE = HEAD_DIM ** -0.5 * math.log2(math.e)

V7X_VMEM_BYTES = 64 * 1024 * 1024
VMEM_LIMIT = V7X_VMEM_BYTES - 8 * 1024 * 1024
V7X_MXU_DIM = 256
BF16_SUBLANES = 16

ADA_ROWS = 16
ADA_TN = 1536
QKV_TILE = 256
FFN_TILE = 512
FF_SPLITS = (0, 6 * V7X_MXU_DIM, D_FF)
SAMPLE_Q_TILE = 512
SAMPLE_HEADS = 1
SAMPLE_Q_SUB = 256
MOE_BLOCK = 512
MOE_ROWS = 2 * MOE_BLOCK + 128
EXPERT_TILE = 256
SEG_BITS = (512, 256, 128, 64, 32, 16)

_NT = (((1,), (1,)), ((), ()))


def _dot(a, b):
    return jnp.dot(a, b, preferred_element_type=F32)


def _split_bf16(x):
    hi = x.astype(BF16)
    lo = (x - hi.astype(F32)).astype(BF16)
    return hi, lo


def _rms(x, g, eps):
    return x * lax.rsqrt(jnp.mean(x * x, axis=-1, keepdims=True) + eps) * g


def _cparams(n_grid, vmem=VMEM_LIMIT):
    return pltpu.CompilerParams(dimension_semantics=("arbitrary",) * n_grid, vmem_limit_bytes=vmem)


def _const_spec(shape):
    nd = len(shape)
    return pl.BlockSpec(shape, lambda *_: (0,) * nd, pipeline_mode=pl.Buffered(1))


def _adaln_kernel(c_ref, w_ref, b_ref, o_ref):
    c = c_ref[...]
    a_hi, a_lo = _split_bf16(c * jax.nn.sigmoid(c))
    w_hi, w_lo = _split_bf16(w_ref[...])
    o_ref[...] = _dot(a_hi, w_hi) + _dot(a_hi, w_lo) + _dot(a_lo, w_hi) + b_ref[...]


def _adaln(cond, w, b):
    n = N_MOD * D_MODEL
    out = pl.pallas_call(
        _adaln_kernel,
        grid=(n // ADA_TN,),
        in_specs=[pl.BlockSpec((ADA_ROWS, D_MODEL), lambda j: (0, 0)),
                  pl.BlockSpec((D_MODEL, ADA_TN), lambda j: (0, j)),
                  pl.BlockSpec((1, ADA_TN), lambda j: (0, j))],
        out_specs=pl.BlockSpec((ADA_ROWS, ADA_TN), lambda j: (0, j)),
        out_shape=jax.ShapeDtypeStruct((ADA_ROWS, n), F32),
        compiler_params=_cparams(1),
        name="adaln",
    )(cond, w, b.reshape(1, n))
    return out.reshape(ADA_ROWS, N_MOD, D_MODEL)


def _mod_spec(seq_tiles, row0):
    if row0 == 0:
        return pl.BlockSpec((1, N_MOD, D_MODEL), lambda i, *_: (0, 0, 0))
    return pl.BlockSpec((1, N_MOD, D_MODEL), lambda i, *_: (row0 + i // seq_tiles, 0, 0))


def _qkv_kernel(*refs, rope):
    if rope:
        x_ref, mod_ref, g_ref, w_ref, cos_ref, sa_ref, sb_ref, q_ref, k_ref, v_ref = refs
    else:
        x_ref, mod_ref, g_ref, w_ref, q_ref, k_ref, v_ref = refs
    h = _rms(x_ref[...], g_ref[...], EPS)
    h = h * (1.0 + mod_ref[0, 1:2, :]) + mod_ref[0, 0:1, :]
    qkv = _dot(h.astype(BF16), w_ref[...])
    inner = N_HEADS * 2 * HEAD_DIM
    for which, ref in ((0, q_ref), (1, k_ref), (2, v_ref)):
        part = qkv[:, which * inner:(which + 1) * inner]
        if rope and which < 2:
            cos, sa, sb = cos_ref[...], sa_ref[...], sb_ref[...]
            for hd in range(N_HEADS):
                blk = part[:, hd * V_DIM:(hd + 1) * V_DIM]
                blk = (blk * cos + pltpu.roll(blk, V_DIM - AXIS_DIM // 2, 1) * sa
                       + pltpu.roll(blk, AXIS_DIM // 2, 1) * sb)
                if which == 0:
                    blk = blk * Q_SCALE
                ref[:, hd * V_DIM:(hd + 1) * V_DIM] = blk.astype(ref.dtype)
        else:
            if which == 0:
                part = part * Q_SCALE
            ref[...] = part.astype(ref.dtype)


def _rope_tables(length):
    rows = length // GRID_W
    row = jnp.repeat(jnp.arange(rows), GRID_W).astype(F32)
    col = jnp.tile(jnp.arange(GRID_W), rows).astype(F32)
    inv = 1.0 / (ROPE_THETA ** (jnp.arange(0, AXIS_DIM, 2, dtype=F32) / AXIS_DIM))
    ar = row[:, None] * inv[None, :]
    ac = col[:, None] * inv[None, :]
    ang = jnp.concatenate([ar, ar, ac, ac], axis=-1)
    cos, sin = jnp.cos(ang), jnp.sin(ang)
    first = (jnp.arange(HEAD_DIM) % AXIS_DIM) < (AXIS_DIM // 2)
    sin_a = jnp.where(first[None, :], -sin, 0.0)
    sin_b = jnp.where(first[None, :], 0.0, sin)
    wide = lambda t: jnp.concatenate([t, t], axis=-1)
    return wide(cos), wide(sin_a), wide(sin_b)


def _qkv(x2d, mod, g, w_bf16, seq_len, row0, rope, kv_dtype):
    t = x2d.shape[0]
    tm = QKV_TILE
    seq_tiles = seq_len // tm
    row_spec = pl.BlockSpec((tm, D_MODEL), lambda i: (i, 0))
    in_specs = [row_spec, _mod_spec(seq_tiles, row0), _const_spec((1, D_MODEL)),
                _const_spec((D_MODEL, 3 * D_MODEL))]
    args = [x2d, mod, g.reshape(1, D_MODEL), w_bf16]
    if rope:
        tab_spec = pl.BlockSpec((tm, V_DIM), lambda i: (i % seq_tiles, 0))
        in_specs += [tab_spec] * 3
        args += list(_rope_tables(seq_len))
    return pl.pallas_call(
        functools.partial(_qkv_kernel, rope=rope),
        grid=(t // tm,),
        in_specs=in_specs,
        out_specs=[row_spec] * 3,
        out_shape=[jax.ShapeDtypeStruct((t, D_MODEL), BF16),
                   jax.ShapeDtypeStruct((t, D_MODEL), kv_dtype),
                   jax.ShapeDtypeStruct((t, D_MODEL), kv_dtype)],
        compiler_params=_cparams(1),
        name="qkv",
    )(*args)


def _attn_kernel(*refs, has_cache, heads, sub):
    if has_cache:
        lam_ref, sg_ref, q_ref, k_ref, v_ref, ck_ref, cv_ref, o_ref = refs
    else:
        lam_ref, sg_ref, q_ref, k_ref, v_ref, o_ref = refs
    lp = lam_ref[...]
    lam = (jnp.exp(jnp.sum(lp[0:1] * lp[1:2], axis=-1, keepdims=True))
           - jnp.exp(jnp.sum(lp[2:3] * lp[3:4], axis=-1, keepdims=True)) + LAMBDA_INIT_0)
    lane = lax.broadcasted_iota(jnp.int32, (1, V_DIM), 1)
    sg = sg_ref[...] * (1.0 - LAMBDA_INIT_0)
    for hd in range(heads):
        cols = slice(hd * V_DIM, (hd + 1) * V_DIM)
        keys = [(k_ref[0, :, cols].astype(BF16), v_ref[0, :, cols].astype(BF16))]
        if has_cache:
            keys.append((ck_ref[0, :, cols].astype(BF16), cv_ref[0, :, cols].astype(BF16)))
        for r0 in range(0, q_ref.shape[0], sub):
            q = q_ref[r0:r0 + sub, cols]
            zero = jnp.zeros_like(q)
            outs = []
            for comp in range(2):
                sel = (lane < HEAD_DIM) if comp == 0 else (lane >= HEAD_DIM)
                qc = jnp.where(sel, q, zero)
                s = [lax.dot_general(qc, k, _NT, preferred_element_type=F32) for k, _ in keys]
                m = functools.reduce(jnp.maximum, [jnp.max(x, axis=-1, keepdims=True) for x in s])
                p = [jnp.exp2(x - m) for x in s]
                l = functools.reduce(jnp.add, [jnp.sum(x, axis=-1, keepdims=True) for x in p])
                o = functools.reduce(jnp.add, [_dot(x.astype(BF16), v) for x, (_, v) in zip(p, keys)])
                outs.append(o * (1.0 / l))
            o = outs[0] - lam * outs[1]
            o_ref[r0:r0 + sub, cols] = _rms(o, sg, SUBLN_EPS).astype(o_ref.dtype)


def _attention(lam_params, subln_g, q, k, v, batch, seq_len, cache_k=None, cache_v=None, *,
               tq, heads, sub):
    nq = seq_len // tq
    has_cache = cache_k is not None
    width = heads * V_DIM
    k3 = k.reshape(batch, seq_len, D_MODEL)
    v3 = v.reshape(batch, seq_len, D_MODEL)
    q_spec = pl.BlockSpec((tq, width), lambda b, h, i: (b * nq + i, h))
    kv_spec = pl.BlockSpec((1, seq_len, width), lambda b, h, i: (b, 0, h))
    in_specs = [_const_spec((4, HEAD_DIM)), _const_spec((1, V_DIM)), q_spec, kv_spec, kv_spec]
    args = [lam_params, subln_g.reshape(1, V_DIM), q, k3, v3]
    if has_cache:
        past = cache_k.shape[1]
        c_spec = pl.BlockSpec((1, past, width), lambda b, h, i: (b, 0, h))
        in_specs += [c_spec, c_spec]
        args += [cache_k.reshape(batch, past, D_MODEL), cache_v.reshape(batch, past, D_MODEL)]
    return pl.pallas_call(
        functools.partial(_attn_kernel, has_cache=has_cache, heads=heads, sub=sub),
        grid=(batch, N_HEADS // heads, nq),
        in_specs=in_specs,
        out_specs=q_spec,
        out_shape=jax.ShapeDtypeStruct((batch * seq_len, D_MODEL), BF16),
        compiler_params=_cparams(3),
        name="diff_attn",
    )(*args)


def _swiglu(h, wgu, wd):
    out = None
    for c0, c1 in zip(FF_SPLITS[:-1], FF_SPLITS[1:]):
        g = _dot(h, wgu[:, c0:c1])
        u = _dot(h, wgu[:, D_FF + c0:D_FF + c1])
        down = _dot((g * jax.nn.sigmoid(g) * u).astype(BF16), wd[c0:c1, :])
        out = down if out is None else out + down
    return out


def _dft_tables(n):
    j = np.arange(n, dtype=np.int64)
    ang = (2.0 * np.pi / n) * ((j[:, None] * j[None, :]) % n).astype(np.float64)
    s = n ** -0.5
    return np.cos(ang) * s, np.sin(ang) * s


def _post_attn_kernel(x_ref, o_ref, mod0_ref, mod1_ref, g2_ref, g1n_ref, wo_ref, wgu_ref, wd_ref,
                      cs_ref, out_ref, a_ref, b_ref):
    x1 = x_ref[...] + mod0_ref[0, 2:3, :] * _dot(o_ref[...], wo_ref[...])
    h = _rms(x1, g2_ref[...], EPS) * (1.0 + mod0_ref[0, 4:5, :]) + mod0_ref[0, 3:4, :]
    x2 = x1 + mod0_ref[0, 5:6, :] * _swiglu(h.astype(BF16), wgu_ref, wd_ref)
    out_ref[...] = x2
    h1 = _rms(x2, g1n_ref[...], EPS) * (1.0 + mod1_ref[0, 1:2, :]) + mod1_ref[0, 0:1, :]
    hb = h1.astype(BF16)
    fg = FOURIER_GROUP
    for grp in range(N_FOURIER_GROUPS):
        ab = _dot(hb[:, grp * fg:(grp + 1) * fg], cs_ref[...])
        a_ref[:, grp * fg:(grp + 1) * fg] = ab[:, :fg].astype(BF16)
        b_ref[:, grp * fg:(grp + 1) * fg] = ab[:, fg:].astype(BF16)


def _post_attn(x2d, o, mod0, mod1, g2, g1n, wo, wgu, wd, seq_len, row0):
    t = x2d.shape[0]
    tm = FFN_TILE if row0 == 0 else min(FFN_TILE, seq_len)
    cd, sd = _dft_tables(FOURIER_GROUP)
    cs = jnp.asarray(np.concatenate([cd, sd], axis=1), dtype=F32).astype(BF16)
    row_spec = pl.BlockSpec((tm, D_MODEL), lambda i: (i, 0))
    mod_spec = _mod_spec(seq_len // tm, row0)
    return pl.pallas_call(
        _post_attn_kernel,
        grid=(t // tm,),
        in_specs=[row_spec, row_spec, mod_spec, mod_spec, _const_spec((1, D_MODEL)),
                  _const_spec((1, D_MODEL)), _const_spec((D_MODEL, D_MODEL)),
                  _const_spec((D_MODEL, 2 * D_FF)), _const_spec((D_FF, D_MODEL)),
                  _const_spec((FOURIER_GROUP, 2 * FOURIER_GROUP))],
        out_specs=[row_spec] * 3,
        out_shape=[jax.ShapeDtypeStruct((t, D_MODEL), F32),
                   jax.ShapeDtypeStruct((t, D_MODEL), BF16),
                   jax.ShapeDtypeStruct((t, D_MODEL), BF16)],
        compiler_params=_cparams(1),
        name="post_attn_swiglu",
    )(x2d, o, mod0, mod1, g2.reshape(1, D_MODEL), g1n.reshape(1, D_MODEL), wo, wgu, wd, cs)


def _route(x, mod_ref, g_ref, wr_ref, h_ref, route_ref, cnt_ref):
    tb = MOE_BLOCK
    h = _rms(x, g_ref[...], EPS) * (1.0 + mod_ref[0, 4:5, :]) + mod_ref[0, 3:4, :]
    h_hi, h_lo = _split_bf16(h)
    h_ref[...] = h_hi
    w_hi, w_lo = _split_bf16(wr_ref[...])
    dg = lambda a, b: lax.dot_general(a, b, _NT, preferred_element_type=F32)
    logits = dg(w_hi, h_hi) + dg(w_lo, h_hi) + dg(w_hi, h_lo)
    e = jnp.exp(logits - jnp.max(logits, axis=0, keepdims=True))
    probs = e / jnp.sum(e, axis=0, keepdims=True)
    eidx = lax.broadcasted_iota(jnp.int32, (N_EXPERTS, tb), 0).astype(F32)
    big = float(N_EXPERTS)
    p1 = jnp.max(probs, axis=0, keepdims=True)
    i1 = jnp.min(jnp.where(probs == p1, eidx, big), axis=0, keepdims=True)
    oh1 = eidx == i1
    rest = jnp.where(oh1, -1.0, probs)
    p2 = jnp.max(rest, axis=0, keepdims=True)
    i2 = jnp.min(jnp.where(rest == p2, eidx, big), axis=0, keepdims=True)
    oh2 = eidx == i2
    den = p1 + p2
    oh = jnp.where(oh1 | oh2, 1.0, 0.0)
    before = (lax.broadcasted_iota(jnp.int32, (tb, tb), 0)
              < lax.broadcasted_iota(jnp.int32, (tb, tb), 1))
    rank = _dot(oh.astype(BF16), jnp.where(before, 1.0, 0.0).astype(BF16))
    cnt = jnp.sum(oh, axis=1, keepdims=True)
    cnt_ref[0] = jnp.broadcast_to(cnt, (N_EXPERTS, 128))
    n16 = jnp.floor((cnt + (BF16_SUBLANES - 1.0)) * (1.0 / BF16_SUBLANES)) * BF16_SUBLANES
    ecol = lax.broadcasted_iota(jnp.int32, (N_EXPERTS, 1), 0)
    seg = jnp.zeros_like(n16)
    for ex in range(N_EXPERTS - 1):
        seg = seg + jnp.where(ecol > ex, n16[ex:ex + 1, :], 0.0)
    pos = rank + seg
    r1 = jnp.sum(jnp.where(oh1, pos, 0.0), axis=0, keepdims=True)
    r2 = jnp.sum(jnp.where(oh2, pos, 0.0), axis=0, keepdims=True)
    zero = jnp.zeros_like(r1)
    route_ref[...] = jnp.concatenate([i1, i2, p1 / den, p2 / den, r1, r2, zero, zero], axis=0)


def _fourier_router_kernel(x_ref, mod_ref, cl_ref, sl_ref, a_ref, b_ref, wf_ref, g_ref, wr_ref,
                           out_ref, h_ref, route_ref, cnt_ref):
    ys = [_dot(cl_ref[...], a_ref[j]) + _dot(sl_ref[...], b_ref[j]) for j in range(a_ref.shape[0])]
    y = ys[0] if len(ys) == 1 else jnp.concatenate(ys, axis=0)
    x = x_ref[...] + mod_ref[0, 2:3, :] * _dot(y.astype(BF16), wf_ref[...])
    out_ref[...] = x
    _route(x, mod_ref, g_ref, wr_ref, h_ref, route_ref, cnt_ref)


def _fourier_router(x2d, a, b, mod, wf, g, wr_t, batch, seq_len, row0):
    t = x2d.shape[0]
    rows = MOE_BLOCK
    part = min(seq_len, rows)
    nbat = rows // part
    nr = seq_len // part
    cl, sl = _dft_tables(seq_len)
    cl = jnp.asarray(cl, dtype=F32).astype(BF16)
    sl = jnp.asarray(-sl, dtype=F32).astype(BF16)
    row_spec = pl.BlockSpec((rows, D_MODEL), lambda i: (i, 0))
    tab_spec = pl.BlockSpec((part, seq_len), lambda i: (i % nr, 0))
    ab_spec = pl.BlockSpec((nbat, seq_len, D_MODEL), lambda i: (i // nr, 0, 0))
    nb = t // rows
    return pl.pallas_call(
        _fourier_router_kernel,
        grid=(nb,),
        in_specs=[row_spec, _mod_spec(nr, row0), tab_spec, tab_spec, ab_spec, ab_spec,
                  _const_spec((D_MODEL, D_MODEL)), _const_spec((1, D_MODEL)),
                  _const_spec((N_EXPERTS, D_MODEL))],
        out_specs=[row_spec, row_spec, pl.BlockSpec((8, rows), lambda i: (0, i)),
                   pl.BlockSpec((1, N_EXPERTS, 128), lambda i: (i, 0, 0))],
        out_shape=[jax.ShapeDtypeStruct((t, D_MODEL), F32),
                   jax.ShapeDtypeStruct((t, D_MODEL), BF16),
                   jax.ShapeDtypeStruct((8, t), F32),
                   jax.ShapeDtypeStruct((nb, N_EXPERTS, 128), F32)],
        compiler_params=_cparams(1),
        name="fourier_router",
    )(x2d, mod, cl, sl, a.reshape(batch, seq_len, D_MODEL), b.reshape(batch, seq_len, D_MODEL),
      wf, g.reshape(1, D_MODEL), wr_t)


def _segment_copies(n16, src_row, dst_row, make_copy):
    for bit in SEG_BITS:
        done = n16 & ~(2 * bit - 1)

        @pl.when((n16 & bit) != 0)
        def _():
            make_copy(pl.multiple_of(src_row + done, BF16_SUBLANES),
                      pl.multiple_of(dst_row + done, BF16_SUBLANES), bit)


def _dispatch_kernel(seg_ref, dst_ref, n16_ref, pad_dst_ref, pad_n_ref, nt_ref, *refs,
                     group_blocks, min_tiles):
    n_in = 2 * len(group_blocks)
    xs_ref, comp_ref, zero_ref, sem = refs[n_in:]
    i = pl.program_id(0)
    last = pl.num_programs(0) - 1
    slot = i % 2
    base = i * N_EXPERTS

    def block_copies(base_, slot_, act):
        for e in range(N_EXPERTS):
            _segment_copies(
                n16_ref[base_ + e], seg_ref[base_ + e], dst_ref[base_ + e],
                lambda s, d, n: act(pltpu.make_async_copy(
                    comp_ref.at[slot_, pl.ds(s, n)], xs_ref.at[pl.ds(d, n)], sem.at[slot_])))

    def zero_copies(act):
        for e in range(N_EXPERTS):
            _segment_copies(
                pad_n_ref[e], 0, pad_dst_ref[e],
                lambda s, d, n: act(pltpu.make_async_copy(
                    zero_ref.at[pl.ds(s, n)], xs_ref.at[pl.ds(d, n)], sem.at[2])))
        for tile in range(min_tiles, xs_ref.shape[0] // EXPERT_TILE):
            @pl.when(tile >= nt_ref[0])
            def _():
                act(pltpu.make_async_copy(
                    zero_ref, xs_ref.at[pl.ds(tile * EXPERT_TILE, EXPERT_TILE)], sem.at[2]))

    @pl.when(i == 0)
    def _():
        zero_ref[...] = jnp.zeros_like(zero_ref)
        zero_copies(lambda c: c.start())

    first = 0
    for grp, nb in enumerate(group_blocks):
        h_ref, route_ref = refs[2 * grp], refs[2 * grp + 1]

        @pl.when((i >= first) & (i < first + nb))
        def _():
            r = route_ref[...]
            pos1, pos2 = r[4:5], r[5:6]
            rows = lax.broadcasted_iota(jnp.int32, (MOE_ROWS, MOE_BLOCK), 0).astype(F32)
            onehot = jnp.where((rows == pos1) | (rows == pos2), 1.0, 0.0).astype(BF16)
            comp_ref[slot] = _dot(onehot, h_ref[...]).astype(BF16)
        first += nb
    block_copies(base, slot, lambda c: c.start())

    @pl.when(i > 0)
    def _():
        block_copies(base - N_EXPERTS, 1 - slot, lambda c: c.wait())

    @pl.when(i == last)
    def _():
        block_copies(base, slot, lambda c: c.wait())
        zero_copies(lambda c: c.wait())


def _dispatch(tables, hs, routes, n_rows, min_tiles):
    group_blocks = tuple(h.shape[0] // MOE_BLOCK for h in hs)
    in_specs, args, first = [], [], 0
    for h, route, nb in zip(hs, routes, group_blocks):
        blk = lambda i, first=first, nb=nb: jnp.clip(i - first, 0, nb - 1)
        in_specs += [pl.BlockSpec((MOE_BLOCK, D_MODEL), lambda i, *_, blk=blk: (blk(i), 0)),
                     pl.BlockSpec((8, MOE_BLOCK), lambda i, *_, blk=blk: (0, blk(i)))]
        args += [h, route]
        first += nb
    grid_spec = pltpu.PrefetchScalarGridSpec(
        num_scalar_prefetch=len(tables),
        grid=(sum(group_blocks),),
        in_specs=in_specs,
        out_specs=pl.BlockSpec(memory_space=pl.ANY),
        scratch_shapes=[pltpu.VMEM((2, MOE_ROWS, D_MODEL), BF16),
                        pltpu.VMEM((EXPERT_TILE, D_MODEL), BF16),
                        pltpu.SemaphoreType.DMA((3,))],
    )
    return pl.pallas_call(
        functools.partial(_dispatch_kernel, group_blocks=group_blocks, min_tiles=min_tiles),
        grid_spec=grid_spec,
        out_shape=jax.ShapeDtypeStruct((n_rows, D_MODEL), BF16),
        compiler_params=_cparams(1),
        name="moe_dispatch",
    )(*tables, *args)


def _expert_kernel(te_ref, nt_ref, xs_ref, wgu_ref, wd_ref, ys_ref):
    del te_ref
    used = pl.program_id(0) < nt_ref[0]

    @pl.when(used)
    def _():
        ys_ref[...] = _swiglu(xs_ref[...], wgu_ref.at[0], wd_ref.at[0]).astype(ys_ref.dtype)

    @pl.when(jnp.logical_not(used))
    def _():
        ys_ref[...] = jnp.zeros_like(ys_ref)


def _experts(tile_expert, n_tiles, xs, wgu_e, wd_e):
    rows = xs.shape[0]
    tm = EXPERT_TILE
    grid_spec = pltpu.PrefetchScalarGridSpec(
        num_scalar_prefetch=2,
        grid=(rows // tm,),
        in_specs=[pl.BlockSpec((tm, D_MODEL), lambda i, te, nt: (jnp.minimum(i, nt[0] - 1), 0)),
                  pl.BlockSpec((1, D_MODEL, 2 * D_FF), lambda i, te, nt: (te[i], 0, 0)),
                  pl.BlockSpec((1, D_FF, D_MODEL), lambda i, te, nt: (te[i], 0, 0))],
        out_specs=pl.BlockSpec((tm, D_MODEL), lambda i, te, nt: (i, 0)),
    )
    return pl.pallas_call(
        _expert_kernel,
        grid_spec=grid_spec,
        out_shape=jax.ShapeDtypeStruct((rows, D_MODEL), BF16),
        compiler_params=_cparams(1),
        name="moe_experts",
    )(tile_expert, n_tiles, xs, wgu_e, wd_e)


def _combine_kernel(seg_ref, dst_ref, n16_ref, x_ref, rt_ref, mod_ref, g_ref, ys_ref, out_ref,
                    buf_ref, sem, *, block0):
    i = pl.program_id(0)
    slot = i % 2
    base = (block0 + i) * N_EXPERTS

    def block_copies(base_, slot_, act):
        for e in range(N_EXPERTS):
            _segment_copies(
                n16_ref[base_ + e], seg_ref[base_ + e], dst_ref[base_ + e],
                lambda s, d, n: act(pltpu.make_async_copy(
                    ys_ref.at[pl.ds(d, n)], buf_ref.at[slot_, pl.ds(s, n)], sem.at[slot_])))

    def fetch(base_, slot_):
        buf_ref[slot_] = jnp.zeros(buf_ref.shape[1:], buf_ref.dtype)
        block_copies(base_, slot_, lambda c: c.start())

    @pl.when(i == 0)
    def _():
        fetch(base, slot)

    @pl.when(i + 1 < pl.num_programs(0))
    def _():
        fetch(base + N_EXPERTS, 1 - slot)

    rt = rt_ref[...]
    pos1, pos2 = rt[:, 4:5], rt[:, 5:6]
    cols = lax.broadcasted_iota(jnp.int32, (MOE_BLOCK, MOE_ROWS), 1).astype(F32)
    gates = (jnp.where(cols == pos1, rt[:, 2:3], 0.0)
             + jnp.where(cols == pos2, rt[:, 3:4], 0.0)).astype(BF16)
    block_copies(base, slot, lambda c: c.wait())
    x = x_ref[...] + mod_ref[0, 5:6, :] * _dot(gates, buf_ref[slot])
    out_ref[...] = _rms(x, g_ref[...], EPS)


def _combine(tables, x2d, route_t, mod, g, ys, seq_len, row0, block0):
    t = x2d.shape[0]
    tb = MOE_BLOCK
    grid_spec = pltpu.PrefetchScalarGridSpec(
        num_scalar_prefetch=len(tables),
        grid=(t // tb,),
        in_specs=[pl.BlockSpec((tb, D_MODEL), lambda i, *_: (i, 0)),
                  pl.BlockSpec((tb, 8), lambda i, *_: (i, 0)),
                  _mod_spec(max(seq_len // tb, 1), row0),
                  pl.BlockSpec((1, D_MODEL), lambda i, *_: (0, 0)),
                  pl.BlockSpec(memory_space=pl.ANY)],
        out_specs=pl.BlockSpec((tb, D_MODEL), lambda i, *_: (i, 0)),
        scratch_shapes=[pltpu.VMEM((2, MOE_ROWS, D_MODEL), BF16), pltpu.SemaphoreType.DMA((2,))],
    )
    return pl.pallas_call(
        functools.partial(_combine_kernel, block0=block0),
        grid_spec=grid_spec,
        out_shape=jax.ShapeDtypeStruct((t, D_MODEL), F32),
        compiler_params=_cparams(1),
        name="moe_combine",
    )(*tables, x2d, route_t, mod, g.reshape(1, D_MODEL), ys)


def _moe_tables(counts, n_rows):
    pad = BF16_SUBLANES
    n16 = (counts + pad - 1) // pad * pad
    seg = jnp.cumsum(n16, axis=1) - n16
    total = jnp.sum(n16, axis=0)
    region = (total + EXPERT_TILE - 1) // EXPERT_TILE * EXPERT_TILE
    region_end = jnp.cumsum(region)
    region_start = region_end - region
    dst = region_start[None, :] + jnp.cumsum(n16, axis=0) - n16
    tiles_end = region_end // EXPERT_TILE
    tile_ids = jnp.arange(n_rows // EXPERT_TILE, dtype=jnp.int32)
    tile_expert = jnp.minimum(jnp.sum(tile_ids[:, None] >= tiles_end[None, :], axis=1), N_EXPERTS - 1)
    flat = lambda a: a.reshape(-1).astype(jnp.int32)
    block_tables = (flat(seg), flat(dst), flat(n16))
    pad_tables = (flat(region_start + total), flat(region - total))
    return block_tables, pad_tables, tile_expert.astype(jnp.int32), tiles_end[-1:].astype(jnp.int32)


def kernel(x_prompt, x_sample, c, cache_k_0, cache_v_0, c_ctx, ada_w_0, ada_b_0, norm1_g_0, norm2_g_0, w_qkv_0, lambda_q1_0, lambda_k1_0, lambda_q2_0, lambda_k2_0, subln_g_0, w_o_0, w_gu_0, w_down_0, ada_w_1, ada_b_1, norm1_g_1, norm2_g_1, w_fourier_1, w_router_1, w_gu_e_1, w_down_e_1, final_norm_g):
    bp, lp, _ = x_prompt.shape
    bs, ls, _ = x_sample.shape
    assert 1 + bs <= ADA_ROWS and (bp * lp) % MOE_BLOCK == 0 and ls % MOE_BLOCK == 0
    assert MOE_BLOCK % lp == 0 or lp % MOE_BLOCK == 0

    cond = jnp.zeros((ADA_ROWS, D_MODEL), F32).at[0].set(c_ctx).at[1:1 + bs].set(c)
    mod0 = _adaln(cond, ada_w_0, ada_b_0)
    mod1 = _adaln(cond, ada_w_1, ada_b_1)
    lam_params = jnp.stack([lambda_q1_0, lambda_k1_0, lambda_q2_0, lambda_k2_0])

    w_qkv = w_qkv_0.astype(BF16)
    w_o = w_o_0.astype(BF16)
    w_gu = w_gu_0.astype(BF16)
    w_down = w_down_0.astype(BF16)
    w_f = w_fourier_1.astype(BF16)
    w_gu_e = w_gu_e_1.astype(BF16)
    w_down_e = w_down_e_1.astype(BF16)
    w_router_t = w_router_1.T

    groups = [dict(x=x_prompt.reshape(bp * lp, D_MODEL), batch=bp, seq=lp, row0=0, rope=False),
              dict(x=x_sample.reshape(bs * ls, D_MODEL), batch=bs, seq=ls, row0=1, rope=True)]

    k_ctx = v_ctx = None
    for gr in groups:
        x, batch, seq, row0 = gr["x"], gr["batch"], gr["seq"], gr["row0"]
        q, k, v = _qkv(x, mod0, norm1_g_0, w_qkv, seq, row0, gr["rope"], BF16 if gr["rope"] else F32)
        if gr["rope"]:
            o = _attention(lam_params, subln_g_0, q, k, v, batch, seq, cache_k_0, cache_v_0,
                           tq=SAMPLE_Q_TILE, heads=SAMPLE_HEADS, sub=SAMPLE_Q_SUB)
        else:
            k_ctx, v_ctx = k, v
            o = _attention(lam_params, subln_g_0, q, k, v, batch, seq, tq=seq, heads=N_HEADS, sub=seq)
        x, fa, fb = _post_attn(x, o, mod0, mod1, norm2_g_0, norm1_g_1, w_o, w_gu, w_down, seq, row0)
        gr["x"], gr["h"], gr["route"], gr["cnt"] = _fourier_router(
            x, fa, fb, mod1, w_f, norm2_g_1, w_router_t, batch, seq, row0)

    n_blocks = [gr["x"].shape[0] // MOE_BLOCK for gr in groups]
    n_pairs = 2 * sum(gr["x"].shape[0] for gr in groups)
    max_rows = n_pairs + sum(n_blocks) * N_EXPERTS * (BF16_SUBLANES - 1) + N_EXPERTS * EXPERT_TILE
    max_rows = (max_rows + EXPERT_TILE - 1) // EXPERT_TILE * EXPERT_TILE
    counts = jnp.concatenate([gr["cnt"][:, :, 0] for gr in groups], axis=0).astype(jnp.int32)
    block_tables, pad_tables, tile_expert, n_tiles = _moe_tables(counts, max_rows)

    xs = _dispatch(block_tables + pad_tables + (n_tiles,), [gr["h"] for gr in groups],
                   [gr["route"] for gr in groups], max_rows, n_pairs // EXPERT_TILE)
    ys = _experts(tile_expert, n_tiles, xs, w_gu_e, w_down_e)
    outs = []
    block0 = 0
    for gr, nb in zip(groups, n_blocks):
        outs.append(_combine(block_tables, gr["x"], gr["route"].T, mod1, final_norm_g, ys,
                             gr["seq"], gr["row0"], block0))
        block0 += nb

    y_prompt = outs[0].reshape(bp, lp, D_MODEL)
    y_sample = outs[1].reshape(bs, ls, D_MODEL)
    return (y_prompt, y_sample,
            k_ctx.reshape(bp, lp, N_HEADS, 2 * HEAD_DIM), v_ctx.reshape(bp, lp, N_HEADS, V_DIM))
```

```python
import functools
import math

import jax
import jax.numpy as jnp
import numpy as np
from jax import lax
from jax.experimental import pallas as pl
from jax.experimental.pallas import tpu as pltpu

F32 = jnp.float32
BF16 = jnp.bfloat16

D_MODEL = 1024
N_HEADS = 8
HEAD_DIM = 64
V_DIM = 2 * HEAD_DIM
GRID_W = 64
AXIS_DIM = HEAD_DIM // 2
ROPE_THETA = 10000.0
N_FOURIER_GROUPS = 4
FOURIER_GROUP = D_MODEL // N_FOURIER_GROUPS
D_FF = 2816
N_EXPERTS = 8
N_MOD = 6
EPS = 1e-6
SUBLN_EPS = 1e-5
LAMBDA_INIT_0 = 0.8 - 0.6 * math.exp(-0.3 * 0)
Q_SCALE = HEAD_DIM ** -0.5 * math.log2(math.e)

V7X_VMEM_BYTES = 64 * 1024 * 1024
VMEM_LIMIT = V7X_VMEM_BYTES - 8 * 1024 * 1024
V7X_MXU_DIM = 256
BF16_SUBLANES = 16

ADA_ROWS = 16
ADA_TN = 1536
QKV_TILE = 512
FFN_TILE = 512
FF_SPLITS = (0, 6 * V7X_MXU_DIM, D_FF)
SAMPLE_Q_TILE = 2048
SAMPLE_HEADS = 1
SAMPLE_Q_SUB = 256
MOE_BLOCK = 512
MOE_ROWS = 2 * MOE_BLOCK + 128
EXPERT_TILE = 256
W_CHUNKS = 16
SEG_BITS = (512, 256, 128, 64, 32, 16)

_NT = (((1,), (1,)), ((), ()))


def _dot(a, b):
    return jnp.dot(a, b, preferred_element_type=F32)


def _split_bf16(x):
    hi = x.astype(BF16)
    lo = (x - hi.astype(F32)).astype(BF16)
    return hi, lo


def _rms(x, g, eps):
    return x * lax.rsqrt(jnp.mean(x * x, axis=-1, keepdims=True) + eps) * g


def _cparams(n_grid, vmem=VMEM_LIMIT):
    return pltpu.CompilerParams(dimension_semantics=("arbitrary",) * n_grid, vmem_limit_bytes=vmem)


def _const_spec(shape):
    nd = len(shape)
    return pl.BlockSpec(shape, lambda *_: (0,) * nd, pipeline_mode=pl.Buffered(1))


def _adaln_kernel(c_ref, w_ref, b_ref, o_ref):
    c = c_ref[...]
    a_hi, a_lo = _split_bf16(c * jax.nn.sigmoid(c))
    w_hi, w_lo = _split_bf16(w_ref[...])
    o_ref[...] = _dot(a_hi, w_hi) + _dot(a_hi, w_lo) + _dot(a_lo, w_hi) + b_ref[...]


def _adaln(cond, w, b):
    n = N_MOD * D_MODEL
    out = pl.pallas_call(
        _adaln_kernel,
        grid=(n // ADA_TN,),
        in_specs=[pl.BlockSpec((ADA_ROWS, D_MODEL), lambda j: (0, 0)),
                  pl.BlockSpec((D_MODEL, ADA_TN), lambda j: (0, j)),
                  pl.BlockSpec((1, ADA_TN), lambda j: (0, j))],
        out_specs=pl.BlockSpec((ADA_ROWS, ADA_TN), lambda j: (0, j)),
        out_shape=jax.ShapeDtypeStruct((ADA_ROWS, n), F32),
        compiler_params=_cparams(1),
        name="adaln",
    )(cond, w, b.reshape(1, n))
    return out.reshape(ADA_ROWS, N_MOD, D_MODEL)


def _mod_spec(seq_tiles, row0):
    if row0 == 0:
        return pl.BlockSpec((1, N_MOD, D_MODEL), lambda i, *_: (0, 0, 0))
    return pl.BlockSpec((1, N_MOD, D_MODEL), lambda i, *_: (row0 + i // seq_tiles, 0, 0))


def _qkv_kernel(*refs, rope):
    if rope:
        x_ref, mod_ref, g_ref, w_ref, cos_ref, sa_ref, sb_ref, q_ref, k_ref, v_ref = refs
    else:
        x_ref, mod_ref, g_ref, w_ref, q_ref, k_ref, v_ref = refs
    h = _rms(x_ref[...], g_ref[...], EPS)
    h = h * (1.0 + mod_ref[0, 1:2, :]) + mod_ref[0, 0:1, :]
    qkv = _dot(h.astype(BF16), w_ref[...])
    inner = N_HEADS * 2 * HEAD_DIM
    for which, ref in ((0, q_ref), (1, k_ref), (2, v_ref)):
        part = qkv[:, which * inner:(which + 1) * inner]
        if rope and which < 2:
            cos, sa, sb = cos_ref[...], sa_ref[...], sb_ref[...]
            for hd in range(N_HEADS):
                blk = part[:, hd * V_DIM:(hd + 1) * V_DIM]
                blk = (blk * cos + pltpu.roll(blk, V_DIM - AXIS_DIM // 2, 1) * sa
                       + pltpu.roll(blk, AXIS_DIM // 2, 1) * sb)
                if which == 0:
                    blk = blk * Q_SCALE
                ref[:, hd * V_DIM:(hd + 1) * V_DIM] = blk.astype(ref.dtype)
        else:
            if which == 0:
                part = part * Q_SCALE
            ref[...] = part.astype(ref.dtype)


def _rope_tables(length):
    rows = length // GRID_W
    row = jnp.repeat(jnp.arange(rows), GRID_W).astype(F32)
    col = jnp.tile(jnp.arange(GRID_W), rows).astype(F32)
    inv = 1.0 / (ROPE_THETA ** (jnp.arange(0, AXIS_DIM, 2, dtype=F32) / AXIS_DIM))
    ar = row[:, None] * inv[None, :]
    ac = col[:, None] * inv[None, :]
    ang = jnp.concatenate([ar, ar, ac, ac], axis=-1)
    cos, sin = jnp.cos(ang), jnp.sin(ang)
    first = (jnp.arange(HEAD_DIM) % AXIS_DIM) < (AXIS_DIM // 2)
    sin_a = jnp.where(first[None, :], -sin, 0.0)
    sin_b = jnp.where(first[None, :], 0.0, sin)
    wide = lambda t: jnp.concatenate([t, t], axis=-1)
    return wide(cos), wide(sin_a), wide(sin_b)


def _qkv(x2d, mod, g, w_bf16, seq_len, row0, rope, kv_dtype):
    t = x2d.shape[0]
    tm = QKV_TILE
    seq_tiles = seq_len // tm
    row_spec = pl.BlockSpec((tm, D_MODEL), lambda i: (i, 0))
    in_specs = [row_spec, _mod_spec(seq_tiles, row0), _const_spec((1, D_MODEL)),
                _const_spec((D_MODEL, 3 * D_MODEL))]
    args = [x2d, mod, g.reshape(1, D_MODEL), w_bf16]
    if rope:
        tab_spec = pl.BlockSpec((tm, V_DIM), lambda i: (i % seq_tiles, 0))
        in_specs += [tab_spec] * 3
        args += list(_rope_tables(seq_len))
    return pl.pallas_call(
        functools.partial(_qkv_kernel, rope=rope),
        grid=(t // tm,),
        in_specs=in_specs,
        out_specs=[row_spec] * 3,
        out_shape=[jax.ShapeDtypeStruct((t, D_MODEL), BF16),
                   jax.ShapeDtypeStruct((t, D_MODEL), kv_dtype),
                   jax.ShapeDtypeStruct((t, D_MODEL), kv_dtype)],
        compiler_params=_cparams(1),
        name="qkv",
    )(*args)


def _attn_kernel(*refs, has_cache, heads, sub):
    if has_cache:
        lam_ref, sg_ref, q_ref, k_ref, v_ref, ck_ref, cv_ref, o_ref = refs
    else:
        lam_ref, sg_ref, q_ref, k_ref, v_ref, o_ref = refs
    lp = lam_ref[...]
    lam = (jnp.exp(jnp.sum(lp[0:1] * lp[1:2], axis=-1, keepdims=True))
           - jnp.exp(jnp.sum(lp[2:3] * lp[3:4], axis=-1, keepdims=True)) + LAMBDA_INIT_0)
    lane = lax.broadcasted_iota(jnp.int32, (1, V_DIM), 1)
    sg = sg_ref[...] * (1.0 - LAMBDA_INIT_0)
    for hd in range(heads):
        cols = slice(hd * V_DIM, (hd + 1) * V_DIM)
        keys = [(k_ref[0, :, cols].astype(BF16), v_ref[0, :, cols].astype(BF16))]
        if has_cache:
            keys.append((ck_ref[0, :, cols].astype(BF16), cv_ref[0, :, cols].astype(BF16)))
        for r0 in range(0, q_ref.shape[0], sub):
            q = q_ref[r0:r0 + sub, cols]
            zero = jnp.zeros_like(q)
            outs = []
            for comp in range(2):
                sel = (lane < HEAD_DIM) if comp == 0 else (lane >= HEAD_DIM)
                qc = jnp.where(sel, q, zero)
                s = [lax.dot_general(qc, k, _NT, preferred_element_type=F32) for k, _ in keys]
                m = functools.reduce(jnp.maximum, [jnp.max(x, axis=-1, keepdims=True) for x in s])
                p = [jnp.exp2(x - m) for x in s]
                l = functools.reduce(jnp.add, [jnp.sum(x, axis=-1, keepdims=True) for x in p])
                o = functools.reduce(jnp.add, [_dot(x.astype(BF16), v) for x, (_, v) in zip(p, keys)])
                outs.append(o * (1.0 / l))
            o = outs[0] - lam * outs[1]
            o_ref[r0:r0 + sub, cols] = _rms(o, sg, SUBLN_EPS).astype(o_ref.dtype)


def _attention(lam_params, subln_g, q, k, v, batch, seq_len, cache_k=None, cache_v=None, *,
               tq, heads, sub):
    nq = seq_len // tq
    has_cache = cache_k is not None
    width = heads * V_DIM
    k3 = k.reshape(batch, seq_len, D_MODEL)
    v3 = v.reshape(batch, seq_len, D_MODEL)
    q_spec = pl.BlockSpec((tq, width), lambda b, h, i: (b * nq + i, h))
    kv_spec = pl.BlockSpec((1, seq_len, width), lambda b, h, i: (b, 0, h))
    in_specs = [_const_spec((4, HEAD_DIM)), _const_spec((1, V_DIM)), q_spec, kv_spec, kv_spec]
    args = [lam_params, subln_g.reshape(1, V_DIM), q, k3, v3]
    if has_cache:
        past = cache_k.shape[1]
        c_spec = pl.BlockSpec((1, past, width), lambda b, h, i: (b, 0, h))
        in_specs += [c_spec, c_spec]
        args += [cache_k.reshape(batch, past, D_MODEL), cache_v.reshape(batch, past, D_MODEL)]
    return pl.pallas_call(
        functools.partial(_attn_kernel, has_cache=has_cache, heads=heads, sub=sub),
        grid=(batch, N_HEADS // heads, nq),
        in_specs=in_specs,
        out_specs=q_spec,
        out_shape=jax.ShapeDtypeStruct((batch * seq_len, D_MODEL), BF16),
        compiler_params=_cparams(3),
        name="diff_attn",
    )(*args)


def _swiglu(h, wgu, wd):
    out = None
    for c0, c1 in zip(FF_SPLITS[:-1], FF_SPLITS[1:]):
        g = _dot(h, wgu[:, c0:c1])
        u = _dot(h, wgu[:, D_FF + c0:D_FF + c1])
        down = _dot((g * jax.nn.sigmoid(g) * u).astype(BF16), wd[c0:c1, :])
        out = down if out is None else out + down
    return out


def _dft_tables(n):
    j = np.arange(n, dtype=np.int64)
    ang = (2.0 * np.pi / n) * ((j[:, None] * j[None, :]) % n).astype(np.float64)
    s = n ** -0.5
    return np.cos(ang) * s, np.sin(ang) * s


def _post_attn_kernel(x_ref, o_ref, mod0_ref, mod1_ref, g2_ref, g1n_ref, wo_ref, wgu_ref, wd_ref,
                      cs_ref, out_ref, a_ref, b_ref):
    x1 = x_ref[...] + mod0_ref[0, 2:3, :] * _dot(o_ref[...], wo_ref[...])
    h = _rms(x1, g2_ref[...], EPS) * (1.0 + mod0_ref[0, 4:5, :]) + mod0_ref[0, 3:4, :]
    x2 = x1 + mod0_ref[0, 5:6, :] * _swiglu(h.astype(BF16), wgu_ref, wd_ref)
    out_ref[...] = x2
    h1 = _rms(x2, g1n_ref[...], EPS) * (1.0 + mod1_ref[0, 1:2, :]) + mod1_ref[0, 0:1, :]
    hb = h1.astype(BF16)
    fg = FOURIER_GROUP
    for grp in range(N_FOURIER_GROUPS):
        ab = _dot(hb[:, grp * fg:(grp + 1) * fg], cs_ref[...])
        a_ref[:, grp * fg:(grp + 1) * fg] = ab[:, :fg].astype(BF16)
        b_ref[:, grp * fg:(grp + 1) * fg] = ab[:, fg:].astype(BF16)


def _post_attn(x2d, o, mod0, mod1, g2, g1n, wo, wgu, wd, seq_len, row0):
    t = x2d.shape[0]
    tm = FFN_TILE if row0 == 0 else min(FFN_TILE, seq_len)
    cd, sd = _dft_tables(FOURIER_GROUP)
    cs = jnp.asarray(np.concatenate([cd, sd], axis=1), dtype=F32).astype(BF16)
    row_spec = pl.BlockSpec((tm, D_MODEL), lambda i: (i, 0))
    mod_spec = _mod_spec(seq_len // tm, row0)
    return pl.pallas_call(
        _post_attn_kernel,
        grid=(t // tm,),
        in_specs=[row_spec, row_spec, mod_spec, mod_spec, _const_spec((1, D_MODEL)),
                  _const_spec((1, D_MODEL)), _const_spec((D_MODEL, D_MODEL)),
                  _const_spec((D_MODEL, 2 * D_FF)), _const_spec((D_FF, D_MODEL)),
                  _const_spec((FOURIER_GROUP, 2 * FOURIER_GROUP))],
        out_specs=[row_spec] * 3,
        out_shape=[jax.ShapeDtypeStruct((t, D_MODEL), F32),
                   jax.ShapeDtypeStruct((t, D_MODEL), BF16),
                   jax.ShapeDtypeStruct((t, D_MODEL), BF16)],
        compiler_params=_cparams(1),
        name="post_attn_swiglu",
    )(x2d, o, mod0, mod1, g2.reshape(1, D_MODEL), g1n.reshape(1, D_MODEL), wo, wgu, wd, cs)


def _route(x, mod_ref, g_ref, wr_ref, h_ref, route_ref, cnt_ref):
    tb = MOE_BLOCK
    h = _rms(x, g_ref[...], EPS) * (1.0 + mod_ref[0, 4:5, :]) + mod_ref[0, 3:4, :]
    h_hi, h_lo = _split_bf16(h)
    h_ref[...] = h_hi
    w_hi, w_lo = _split_bf16(wr_ref[...])
    dg = lambda a, b: lax.dot_general(a, b, _NT, preferred_element_type=F32)
    logits = dg(w_hi, h_hi) + dg(w_lo, h_hi) + dg(w_hi, h_lo)
    e = jnp.exp(logits - jnp.max(logits, axis=0, keepdims=True))
    probs = e / jnp.sum(e, axis=0, keepdims=True)
    eidx = lax.broadcasted_iota(jnp.int32, (N_EXPERTS, tb), 0).astype(F32)
    big = float(N_EXPERTS)
    p1 = jnp.max(probs, axis=0, keepdims=True)
    i1 = jnp.min(jnp.where(probs == p1, eidx, big), axis=0, keepdims=True)
    oh1 = eidx == i1
    rest = jnp.where(oh1, -1.0, probs)
    p2 = jnp.max(rest, axis=0, keepdims=True)
    i2 = jnp.min(jnp.where(rest == p2, eidx, big), axis=0, keepdims=True)
    oh2 = eidx == i2
    den = p1 + p2
    oh = jnp.where(oh1 | oh2, 1.0, 0.0)
    before = (lax.broadcasted_iota(jnp.int32, (tb, tb), 0)
              < lax.broadcasted_iota(jnp.int32, (tb, tb), 1))
    rank = _dot(oh.astype(BF16), jnp.where(before, 1.0, 0.0).astype(BF16))
    cnt = jnp.sum(oh, axis=1, keepdims=True)
    cnt_ref[0] = jnp.broadcast_to(cnt, (N_EXPERTS, 128))
    n16 = jnp.floor((cnt + (BF16_SUBLANES - 1.0)) * (1.0 / BF16_SUBLANES)) * BF16_SUBLANES
    ecol = lax.broadcasted_iota(jnp.int32, (N_EXPERTS, 1), 0)
    seg = jnp.zeros_like(n16)
    for ex in range(N_EXPERTS - 1):
        seg = seg + jnp.where(ecol > ex, n16[ex:ex + 1, :], 0.0)
    pos = rank + seg
    r1 = jnp.sum(jnp.where(oh1, pos, 0.0), axis=0, keepdims=True)
    r2 = jnp.sum(jnp.where(oh2, pos, 0.0), axis=0, keepdims=True)
    zero = jnp.zeros_like(r1)
    route_ref[...] = jnp.concatenate([i1, i2, p1 / den, p2 / den, r1, r2, zero, zero], axis=0)


def _fourier_router_kernel(x_ref, mod_ref, cl_ref, sl_ref, a_ref, b_ref, wf_ref, g_ref, wr_ref,
                           out_ref, h_ref, route_ref, cnt_ref):
    ys = [_dot(cl_ref[...], a_ref[j]) + _dot(sl_ref[...], b_ref[j]) for j in range(a_ref.shape[0])]
    y = ys[0] if len(ys) == 1 else jnp.concatenate(ys, axis=0)
    x = x_ref[...] + mod_ref[0, 2:3, :] * _dot(y.astype(BF16), wf_ref[...])
    out_ref[...] = x
    _route(x, mod_ref, g_ref, wr_ref, h_ref, route_ref, cnt_ref)


def _fourier_router(x2d, a, b, mod, wf, g, wr_t, batch, seq_len, row0):
    t = x2d.shape[0]
    rows = MOE_BLOCK
    part = min(seq_len, rows)
    nbat = rows // part
    nr = seq_len // part
    cl, sl = _dft_tables(seq_len)
    cl = jnp.asarray(cl, dtype=F32).astype(BF16)
    sl = jnp.asarray(-sl, dtype=F32).astype(BF16)
    row_spec = pl.BlockSpec((rows, D_MODEL), lambda i: (i, 0))
    tab_spec = pl.BlockSpec((part, seq_len), lambda i: (i % nr, 0))
    ab_spec = pl.BlockSpec((nbat, seq_len, D_MODEL), lambda i: (i // nr, 0, 0))
    nb = t // rows
    return pl.pallas_call(
        _fourier_router_kernel,
        grid=(nb,),
        in_specs=[row_spec, _mod_spec(nr, row0), tab_spec, tab_spec, ab_spec, ab_spec,
                  _const_spec((D_MODEL, D_MODEL)), _const_spec((1, D_MODEL)),
                  _const_spec((N_EXPERTS, D_MODEL))],
        out_specs=[row_spec, row_spec, pl.BlockSpec((8, rows), lambda i: (0, i)),
                   pl.BlockSpec((1, N_EXPERTS, 128), lambda i: (i, 0, 0))],
        out_shape=[jax.ShapeDtypeStruct((t, D_MODEL), F32),
                   jax.ShapeDtypeStruct((t, D_MODEL), BF16),
                   jax.ShapeDtypeStruct((8, t), F32),
                   jax.ShapeDtypeStruct((nb, N_EXPERTS, 128), F32)],
        compiler_params=_cparams(1),
        name="fourier_router",
    )(x2d, mod, cl, sl, a.reshape(batch, seq_len, D_MODEL), b.reshape(batch, seq_len, D_MODEL),
      wf, g.reshape(1, D_MODEL), wr_t)


def _segment_copies(n16, src_row, dst_row, make_copy):
    for bit in SEG_BITS:
        done = n16 & ~(2 * bit - 1)

        @pl.when((n16 & bit) != 0)
        def _():
            make_copy(pl.multiple_of(src_row + done, BF16_SUBLANES),
                      pl.multiple_of(dst_row + done, BF16_SUBLANES), bit)


def _dispatch_kernel(seg_ref, dst_ref, n16_ref, pad_dst_ref, pad_n_ref, nt_ref, *refs,
                     group_blocks, min_tiles):
    n_in = 2 * len(group_blocks)
    xs_ref, comp_ref, zero_ref, sem = refs[n_in:]
    i = pl.program_id(0)
    last = pl.num_programs(0) - 1
    slot = i % 2
    base = i * N_EXPERTS

    def block_copies(base_, slot_, act):
        for e in range(N_EXPERTS):
            _segment_copies(
                n16_ref[base_ + e], seg_ref[base_ + e], dst_ref[base_ + e],
                lambda s, d, n: act(pltpu.make_async_copy(
                    comp_ref.at[slot_, pl.ds(s, n)], xs_ref.at[pl.ds(d, n)], sem.at[slot_])))

    def zero_copies(act):
        for e in range(N_EXPERTS):
            _segment_copies(
                pad_n_ref[e], 0, pad_dst_ref[e],
                lambda s, d, n: act(pltpu.make_async_copy(
                    zero_ref.at[pl.ds(s, n)], xs_ref.at[pl.ds(d, n)], sem.at[2])))
        for tile in range(min_tiles, xs_ref.shape[0] // EXPERT_TILE):
            @pl.when(tile >= nt_ref[0])
            def _():
                act(pltpu.make_async_copy(
                    zero_ref, xs_ref.at[pl.ds(tile * EXPERT_TILE, EXPERT_TILE)], sem.at[2]))

    @pl.when(i == 0)
    def _():
        zero_ref[...] = jnp.zeros_like(zero_ref)
        zero_copies(lambda c: c.start())

    first = 0
    for grp, nb in enumerate(group_blocks):
        h_ref, route_ref = refs[2 * grp], refs[2 * grp + 1]

        @pl.when((i >= first) & (i < first + nb))
        def _():
            r = route_ref[...]
            pos1, pos2 = r[4:5], r[5:6]
            rows = lax.broadcasted_iota(jnp.int32, (MOE_ROWS, MOE_BLOCK), 0).astype(F32)
            onehot = jnp.where((rows == pos1) | (rows == pos2), 1.0, 0.0).astype(BF16)
            comp_ref[slot] = _dot(onehot, h_ref[...]).astype(BF16)
        first += nb
    block_copies(base, slot, lambda c: c.start())

    @pl.when(i > 0)
    def _():
        block_copies(base - N_EXPERTS, 1 - slot, lambda c: c.wait())

    @pl.when(i == last)
    def _():
        block_copies(base, slot, lambda c: c.wait())
        zero_copies(lambda c: c.wait())


def _dispatch(tables, hs, routes, n_rows, min_tiles):
    group_blocks = tuple(h.shape[0] // MOE_BLOCK for h in hs)
    in_specs, args, first = [], [], 0
    for h, route, nb in zip(hs, routes, group_blocks):
        blk = lambda i, first=first, nb=nb: jnp.clip(i - first, 0, nb - 1)
        in_specs += [pl.BlockSpec((MOE_BLOCK, D_MODEL), lambda i, *_, blk=blk: (blk(i), 0)),
                     pl.BlockSpec((8, MOE_BLOCK), lambda i, *_, blk=blk: (0, blk(i)))]
        args += [h, route]
        first += nb
    grid_spec = pltpu.PrefetchScalarGridSpec(
        num_scalar_prefetch=len(tables),
        grid=(sum(group_blocks),),
        in_specs=in_specs,
        out_specs=pl.BlockSpec(memory_space=pl.ANY),
        scratch_shapes=[pltpu.VMEM((2, MOE_ROWS, D_MODEL), BF16),
                        pltpu.VMEM((EXPERT_TILE, D_MODEL), BF16),
                        pltpu.SemaphoreType.DMA((3,))],
    )
    return pl.pallas_call(
        functools.partial(_dispatch_kernel, group_blocks=group_blocks, min_tiles=min_tiles),
        grid_spec=grid_spec,
        out_shape=jax.ShapeDtypeStruct((n_rows, D_MODEL), BF16),
        compiler_params=_cparams(1),
        name="moe_dispatch",
    )(*tables, *args)


def _expert_kernel(te_ref, nt_ref, catch_ref, bge_ref, bgc_ref, xs_ref, wgu_hbm, wd_hbm, ys_ref,
                   wgu_bf, wd_bf, stage_gu, stage_d, sem):
    i = pl.program_id(0)
    used = i < nt_ref[0]
    rg, rd = D_MODEL // W_CHUNKS, D_FF // W_CHUNKS

    def chunk_copies(expert, c, slot):
        return (pltpu.make_async_copy(wgu_hbm.at[expert, pl.ds(pl.multiple_of(c * rg, rg), rg)],
                                      stage_gu.at[slot], sem.at[0, slot]),
                pltpu.make_async_copy(wd_hbm.at[expert, pl.ds(pl.multiple_of(c * rd, rd), rd)],
                                      stage_d.at[slot], sem.at[1, slot]))

    def cast_chunk(expert, c, slot):
        ws = expert % 2
        wgu_bf[ws, pl.ds(pl.multiple_of(c * rg, rg), rg), :] = stage_gu[slot].astype(BF16)
        wd_bf[ws, pl.ds(pl.multiple_of(c * rd, rd), rd), :] = stage_d[slot].astype(BF16)

    @pl.when(used)
    def _():
        e = te_ref[i]

        @pl.when(i == 0)
        def _():
            for cp in chunk_copies(bge_ref[0], bgc_ref[0], 0):
                cp.start()

        @pl.when(i + 1 < nt_ref[0])
        def _():
            for cp in chunk_copies(bge_ref[i + 1], bgc_ref[i + 1], (i + 1) % 2):
                cp.start()

        def catch_up(c, carry):
            cps = chunk_copies(e, c, 2)
            for cp in cps:
                cp.start()
            for cp in cps:
                cp.wait()
            cast_chunk(e, c, 2)
            return carry

        lax.fori_loop(catch_ref[i], W_CHUNKS, catch_up, 0)

        for cp in chunk_copies(bge_ref[i], bgc_ref[i], i % 2):
            cp.wait()
        cast_chunk(bge_ref[i], bgc_ref[i], i % 2)
        ws = e % 2
        ys_ref[...] = _swiglu(xs_ref[...], wgu_bf.at[ws], wd_bf.at[ws]).astype(ys_ref.dtype)

    @pl.when(jnp.logical_not(used))
    def _():
        ys_ref[...] = jnp.zeros_like(ys_ref)


def _experts(tile_tables, xs, wgu_e, wd_e):
    rows = xs.shape[0]
    tm = EXPERT_TILE
    rg, rd = D_MODEL // W_CHUNKS, D_FF // W_CHUNKS
    grid_spec = pltpu.PrefetchScalarGridSpec(
        num_scalar_prefetch=len(tile_tables),
        grid=(rows // tm,),
        in_specs=[pl.BlockSpec((tm, D_MODEL), lambda i, te, nt, *_: (jnp.minimum(i, nt[0] - 1), 0)),
                  pl.BlockSpec(memory_space=pl.ANY), pl.BlockSpec(memory_space=pl.ANY)],
        out_specs=pl.BlockSpec((tm, D_MODEL), lambda i, *_: (i, 0)),
        scratch_shapes=[pltpu.VMEM((2, D_MODEL, 2 * D_FF), BF16), pltpu.VMEM((2, D_FF, D_MODEL), BF16),
                        pltpu.VMEM((3, rg, 2 * D_FF), F32), pltpu.VMEM((3, rd, D_MODEL), F32),
                        pltpu.SemaphoreType.DMA((2, 3))],
    )
    return pl.pallas_call(
        _expert_kernel,
        grid_spec=grid_spec,
        out_shape=jax.ShapeDtypeStruct((rows, D_MODEL), BF16),
        compiler_params=_cparams(1),
        name="moe_experts",
    )(*tile_tables, xs, wgu_e, wd_e)


def _combine_kernel(seg_ref, dst_ref, n16_ref, x_ref, rt_ref, mod_ref, g_ref, ys_ref, out_ref,
                    buf_ref, sem, *, block0):
    i = pl.program_id(0)
    slot = i % 2
    base = (block0 + i) * N_EXPERTS

    def block_copies(base_, slot_, act):
        for e in range(N_EXPERTS):
            _segment_copies(
                n16_ref[base_ + e], seg_ref[base_ + e], dst_ref[base_ + e],
                lambda s, d, n: act(pltpu.make_async_copy(
                    ys_ref.at[pl.ds(d, n)], buf_ref.at[slot_, pl.ds(s, n)], sem.at[slot_])))

    def fetch(base_, slot_):
        buf_ref[slot_] = jnp.zeros(buf_ref.shape[1:], buf_ref.dtype)
        block_copies(base_, slot_, lambda c: c.start())

    @pl.when(i == 0)
    def _():
        fetch(base, slot)

    @pl.when(i + 1 < pl.num_programs(0))
    def _():
        fetch(base + N_EXPERTS, 1 - slot)

    rt = rt_ref[...]
    pos1, pos2 = rt[:, 4:5], rt[:, 5:6]
    cols = lax.broadcasted_iota(jnp.int32, (MOE_BLOCK, MOE_ROWS), 1).astype(F32)
    gates = (jnp.where(cols == pos1, rt[:, 2:3], 0.0)
             + jnp.where(cols == pos2, rt[:, 3:4], 0.0)).astype(BF16)
    block_copies(base, slot, lambda c: c.wait())
    x = x_ref[...] + mod_ref[0, 5:6, :] * _dot(gates, buf_ref[slot])
    out_ref[...] = _rms(x, g_ref[...], EPS)


def _combine(tables, x2d, route_t, mod, g, ys, seq_len, row0, block0):
    t = x2d.shape[0]
    tb = MOE_BLOCK
    grid_spec = pltpu.PrefetchScalarGridSpec(
        num_scalar_prefetch=len(tables),
        grid=(t // tb,),
        in_specs=[pl.BlockSpec((tb, D_MODEL), lambda i, *_: (i, 0)),
                  pl.BlockSpec((tb, 8), lambda i, *_: (i, 0)),
                  _mod_spec(max(seq_len // tb, 1), row0),
                  pl.BlockSpec((1, D_MODEL), lambda i, *_: (0, 0)),
                  pl.BlockSpec(memory_space=pl.ANY)],
        out_specs=pl.BlockSpec((tb, D_MODEL), lambda i, *_: (i, 0)),
        scratch_shapes=[pltpu.VMEM((2, MOE_ROWS, D_MODEL), BF16), pltpu.SemaphoreType.DMA((2,))],
    )
    return pl.pallas_call(
        functools.partial(_combine_kernel, block0=block0),
        grid_spec=grid_spec,
        out_shape=jax.ShapeDtypeStruct((t, D_MODEL), F32),
        compiler_params=_cparams(1),
        name="moe_combine",
    )(*tables, x2d, route_t, mod, g.reshape(1, D_MODEL), ys)


def _moe_tables(counts, n_rows):
    pad = BF16_SUBLANES
    n16 = (counts + pad - 1) // pad * pad
    seg = jnp.cumsum(n16, axis=1) - n16
    total = jnp.sum(n16, axis=0)
    region = (total + EXPERT_TILE - 1) // EXPERT_TILE * EXPERT_TILE
    region_end = jnp.cumsum(region)
    region_start = region_end - region
    dst = region_start[None, :] + jnp.cumsum(n16, axis=0) - n16
    tiles_end = region_end // EXPERT_TILE
    tile_ids = jnp.arange(n_rows // EXPERT_TILE, dtype=jnp.int32)
    tile_expert = jnp.minimum(jnp.sum(tile_ids[:, None] >= tiles_end[None, :], axis=1), N_EXPERTS - 1)
    n_tiles_e = region // EXPERT_TILE
    local = tile_ids - (tiles_end - n_tiles_e)[tile_expert]
    prev_tiles = jnp.concatenate([jnp.zeros((1,), n_tiles_e.dtype), n_tiles_e[:-1]])
    catch_from = jnp.where(local == 0, jnp.minimum(prev_tiles[tile_expert], W_CHUNKS), W_CHUNKS)
    ahead_expert = jnp.where(tile_expert < N_EXPERTS - 1, tile_expert + 1, N_EXPERTS - 2)
    ahead_chunk = jnp.clip(local, 0, W_CHUNKS - 1)
    flat = lambda a: a.reshape(-1).astype(jnp.int32)
    block_tables = (flat(seg), flat(dst), flat(n16))
    pad_tables = (flat(region_start + total), flat(region - total))
    tile_tables = (flat(tile_expert), flat(tiles_end[-1:]), flat(catch_from), flat(ahead_expert),
                   flat(ahead_chunk))
    return block_tables, pad_tables, tile_tables


def kernel(x_prompt, x_sample, c, cache_k_0, cache_v_0, c_ctx, ada_w_0, ada_b_0, norm1_g_0, norm2_g_0, w_qkv_0, lambda_q1_0, lambda_k1_0, lambda_q2_0, lambda_k2_0, subln_g_0, w_o_0, w_gu_0, w_down_0, ada_w_1, ada_b_1, norm1_g_1, norm2_g_1, w_fourier_1, w_router_1, w_gu_e_1, w_down_e_1, final_norm_g):
    bp, lp, _ = x_prompt.shape
    bs, ls, _ = x_sample.shape
    assert 1 + bs <= ADA_ROWS and (bp * lp) % MOE_BLOCK == 0 and ls % MOE_BLOCK == 0
    assert MOE_BLOCK % lp == 0 or lp % MOE_BLOCK == 0

    cond = jnp.zeros((ADA_ROWS, D_MODEL), F32).at[0].set(c_ctx).at[1:1 + bs].set(c)
    mod0 = _adaln(cond, ada_w_0, ada_b_0)
    mod1 = _adaln(cond, ada_w_1, ada_b_1)
    lam_params = jnp.stack([lambda_q1_0, lambda_k1_0, lambda_q2_0, lambda_k2_0])

    w_qkv = w_qkv_0.astype(BF16)
    w_o = w_o_0.astype(BF16)
    w_gu = w_gu_0.astype(BF16)
    w_down = w_down_0.astype(BF16)
    w_f = w_fourier_1.astype(BF16)
    w_router_t = w_router_1.T

    groups = [dict(x=x_prompt.reshape(bp * lp, D_MODEL), batch=bp, seq=lp, row0=0, rope=False),
              dict(x=x_sample.reshape(bs * ls, D_MODEL), batch=bs, seq=ls, row0=1, rope=True)]

    k_ctx = v_ctx = None
    for gr in groups:
        x, batch, seq, row0 = gr["x"], gr["batch"], gr["seq"], gr["row0"]
        q, k, v = _qkv(x, mod0, norm1_g_0, w_qkv, seq, row0, gr["rope"], BF16 if gr["rope"] else F32)
        if gr["rope"]:
            o = _attention(lam_params, subln_g_0, q, k, v, batch, seq, cache_k_0, cache_v_0,
                           tq=SAMPLE_Q_TILE, heads=SAMPLE_HEADS, sub=SAMPLE_Q_SUB)
        else:
            k_ctx, v_ctx = k, v
            o = _attention(lam_params, subln_g_0, q, k, v, batch, seq, tq=seq, heads=N_HEADS, sub=seq)
        x, fa, fb = _post_attn(x, o, mod0, mod1, norm2_g_0, norm1_g_1, w_o, w_gu, w_down, seq, row0)
        gr["x"], gr["h"], gr["route"], gr["cnt"] = _fourier_router(
            x, fa, fb, mod1, w_f, norm2_g_1, w_router_t, batch, seq, row0)

    n_blocks = [gr["x"].shape[0] // MOE_BLOCK for gr in groups]
    n_pairs = 2 * sum(gr["x"].shape[0] for gr in groups)
    max_rows = n_pairs + sum(n_blocks) * N_EXPERTS * (BF16_SUBLANES - 1) + N_EXPERTS * EXPERT_TILE
    max_rows = (max_rows + EXPERT_TILE - 1) // EXPERT_TILE * EXPERT_TILE
    counts = jnp.concatenate([gr["cnt"][:, :, 0] for gr in groups], axis=0).astype(jnp.int32)
    block_tables, pad_tables, tile_tables = _moe_tables(counts, max_rows)

    xs = _dispatch(block_tables + pad_tables + tile_tables[1:2], [gr["h"] for gr in groups],
                   [gr["route"] for gr in groups], max_rows, n_pairs // EXPERT_TILE)
    ys = _experts(tile_tables, xs, w_gu_e_1, w_down_e_1)
    outs = []
    block0 = 0
    for gr, nb in zip(groups, n_blocks):
        outs.append(_combine(block_tables, gr["x"], gr["route"].T, mod1, final_norm_g, ys,
                             gr["seq"], gr["row0"], block0))
        block0 += nb

    y_prompt = outs[0].reshape(bp, lp, D_MODEL)
    y_sample = outs[1].reshape(bs, ls, D_MODEL)
    return (y_prompt, y_sample,
            k_ctx.reshape(bp, lp, N_HEADS, 2 * HEAD_DIM), v_ctx.reshape(bp, lp, N_HEADS, V_DIM))
```

```python
import functools
import math

import jax
import jax.numpy as jnp
import numpy as np
from jax import lax
from jax.experimental import pallas as pl
from jax.experimental.pallas import tpu as pltpu

F32 = jnp.float32
BF16 = jnp.bfloat16

D_MODEL = 1024
N_HEADS = 8
HEAD_DIM = 64
V_DIM = 2 * HEAD_DIM
GRID_W = 64
AXIS_DIM = HEAD_DIM // 2
ROPE_THETA = 10000.0
N_FOURIER_GROUPS = 4
FOURIER_GROUP = D_MODEL // N_FOURIER_GROUPS
D_FF = 2816
N_EXPERTS = 8
N_MOD = 6
EPS = 1e-6
SUBLN_EPS = 1e-5
LAMBDA_INIT_0 = 0.8 - 0.6 * math.exp(-0.3 * 0)
Q_SCALE = HEAD_DIM ** -0.5 * math.log2(math.e)

V7X_VMEM_BYTES = 64 * 1024 * 1024
VMEM_LIMIT = V7X_VMEM_BYTES - 8 * 1024 * 1024
V7X_MXU_DIM = 256
BF16_SUBLANES = 16

ADA_ROWS = 16
ADA_TN = 1536
QKV_TILE = 512
FFN_TILE = 512
FF_SPLITS = (0, 6 * V7X_MXU_DIM, D_FF)
SAMPLE_Q_TILE = 2048
SAMPLE_HEADS = 1
SAMPLE_Q_SUB = 128
ATTN_LAG = 1
MOE_BLOCK = 512
MOE_ROWS = 2 * MOE_BLOCK + 128
EXPERT_TILE = 256
W_CHUNKS = 16
SEG_BITS = (512, 256, 128, 64, 32, 16)

_NT = (((1,), (1,)), ((), ()))


def _dot(a, b):
    return jnp.dot(a, b, preferred_element_type=F32)


def _split_bf16(x):
    hi = x.astype(BF16)
    lo = (x - hi.astype(F32)).astype(BF16)
    return hi, lo


def _rms(x, g, eps):
    return x * lax.rsqrt(jnp.mean(x * x, axis=-1, keepdims=True) + eps) * g


def _cparams(n_grid, vmem=VMEM_LIMIT):
    return pltpu.CompilerParams(dimension_semantics=("arbitrary",) * n_grid, vmem_limit_bytes=vmem)


def _const_spec(shape):
    nd = len(shape)
    return pl.BlockSpec(shape, lambda *_: (0,) * nd, pipeline_mode=pl.Buffered(1))


def _adaln_kernel(c_ref, w_ref, b_ref, o_ref):
    c = c_ref[...]
    a_hi, a_lo = _split_bf16(c * jax.nn.sigmoid(c))
    w_hi, w_lo = _split_bf16(w_ref[...])
    o_ref[...] = _dot(a_hi, w_hi) + _dot(a_hi, w_lo) + _dot(a_lo, w_hi) + b_ref[...]


def _adaln(cond, w, b):
    n = N_MOD * D_MODEL
    out = pl.pallas_call(
        _adaln_kernel,
        grid=(n // ADA_TN,),
        in_specs=[pl.BlockSpec((ADA_ROWS, D_MODEL), lambda j: (0, 0)),
                  pl.BlockSpec((D_MODEL, ADA_TN), lambda j: (0, j)),
                  pl.BlockSpec((1, ADA_TN), lambda j: (0, j))],
        out_specs=pl.BlockSpec((ADA_ROWS, ADA_TN), lambda j: (0, j)),
        out_shape=jax.ShapeDtypeStruct((ADA_ROWS, n), F32),
        compiler_params=_cparams(1),
        name="adaln",
    )(cond, w, b.reshape(1, n))
    return out.reshape(ADA_ROWS, N_MOD, D_MODEL)


def _mod_spec(seq_tiles, row0):
    if row0 == 0:
        return pl.BlockSpec((1, N_MOD, D_MODEL), lambda i, *_: (0, 0, 0))
    return pl.BlockSpec((1, N_MOD, D_MODEL), lambda i, *_: (row0 + i // seq_tiles, 0, 0))


def _qkv_kernel(*refs, rope):
    if rope:
        x_ref, mod_ref, g_ref, w_ref, cos_ref, sa_ref, sb_ref, q_ref, k_ref, v_ref = refs
    else:
        x_ref, mod_ref, g_ref, w_ref, q_ref, k_ref, v_ref = refs
    h = _rms(x_ref[...], g_ref[...], EPS)
    h = h * (1.0 + mod_ref[0, 1:2, :]) + mod_ref[0, 0:1, :]
    qkv = _dot(h.astype(BF16), w_ref[...])
    inner = N_HEADS * 2 * HEAD_DIM
    for which, ref in ((0, q_ref), (1, k_ref), (2, v_ref)):
        part = qkv[:, which * inner:(which + 1) * inner]
        if rope and which < 2:
            cos, sa, sb = cos_ref[...], sa_ref[...], sb_ref[...]
            for hd in range(N_HEADS):
                blk = part[:, hd * V_DIM:(hd + 1) * V_DIM]
                blk = (blk * cos + pltpu.roll(blk, V_DIM - AXIS_DIM // 2, 1) * sa
                       + pltpu.roll(blk, AXIS_DIM // 2, 1) * sb)
                if which == 0:
                    blk = blk * Q_SCALE
                ref[:, hd * V_DIM:(hd + 1) * V_DIM] = blk.astype(ref.dtype)
        else:
            if which == 0:
                part = part * Q_SCALE
            ref[...] = part.astype(ref.dtype)


def _rope_tables(length):
    rows = length // GRID_W
    row = jnp.repeat(jnp.arange(rows), GRID_W).astype(F32)
    col = jnp.tile(jnp.arange(GRID_W), rows).astype(F32)
    inv = 1.0 / (ROPE_THETA ** (jnp.arange(0, AXIS_DIM, 2, dtype=F32) / AXIS_DIM))
    ar = row[:, None] * inv[None, :]
    ac = col[:, None] * inv[None, :]
    ang = jnp.concatenate([ar, ar, ac, ac], axis=-1)
    cos, sin = jnp.cos(ang), jnp.sin(ang)
    first = (jnp.arange(HEAD_DIM) % AXIS_DIM) < (AXIS_DIM // 2)
    sin_a = jnp.where(first[None, :], -sin, 0.0)
    sin_b = jnp.where(first[None, :], 0.0, sin)
    wide = lambda t: jnp.concatenate([t, t], axis=-1)
    return wide(cos), wide(sin_a), wide(sin_b)


def _qkv(x2d, mod, g, w_bf16, seq_len, row0, rope, kv_dtype):
    t = x2d.shape[0]
    tm = QKV_TILE
    seq_tiles = seq_len // tm
    row_spec = pl.BlockSpec((tm, D_MODEL), lambda i: (i, 0))
    in_specs = [row_spec, _mod_spec(seq_tiles, row0), _const_spec((1, D_MODEL)),
                _const_spec((D_MODEL, 3 * D_MODEL))]
    args = [x2d, mod, g.reshape(1, D_MODEL), w_bf16]
    if rope:
        tab_spec = pl.BlockSpec((tm, V_DIM), lambda i: (i % seq_tiles, 0))
        in_specs += [tab_spec] * 3
        args += list(_rope_tables(seq_len))
    return pl.pallas_call(
        functools.partial(_qkv_kernel, rope=rope),
        grid=(t // tm,),
        in_specs=in_specs,
        out_specs=[row_spec] * 3,
        out_shape=[jax.ShapeDtypeStruct((t, D_MODEL), BF16),
                   jax.ShapeDtypeStruct((t, D_MODEL), kv_dtype),
                   jax.ShapeDtypeStruct((t, D_MODEL), kv_dtype)],
        compiler_params=_cparams(1),
        name="qkv",
    )(*args)


def _attn_kernel(*refs, has_cache, heads, sub, lag):
    if has_cache:
        lam_ref, sg_ref, q_ref, k_ref, v_ref, ck_ref, cv_ref, o_ref = refs
    else:
        lam_ref, sg_ref, q_ref, k_ref, v_ref, o_ref = refs
    lp = lam_ref[...]
    lam = (jnp.exp(jnp.sum(lp[0:1] * lp[1:2], axis=-1, keepdims=True))
           - jnp.exp(jnp.sum(lp[2:3] * lp[3:4], axis=-1, keepdims=True)) + LAMBDA_INIT_0)
    lane = lax.broadcasted_iota(jnp.int32, (1, V_DIM), 1)
    sg = sg_ref[...] * (1.0 - LAMBDA_INIT_0)
    def head_keys(hd):
        cols = slice(hd * V_DIM, (hd + 1) * V_DIM)
        keys = [(k_ref[0, :, cols].astype(BF16), v_ref[0, :, cols].astype(BF16))]
        if has_cache:
            keys.append((ck_ref[0, :, cols].astype(BF16), cv_ref[0, :, cols].astype(BF16)))
        return keys

    chains = [(hd, r0, comp) for hd in range(heads) for r0 in range(0, q_ref.shape[0], sub)
              for comp in range(2)]
    keys = {hd: head_keys(hd) for hd in range(heads)}
    scores, probs, outs = {}, {}, {}

    def stage_scores(c):
        hd, r0, comp = chains[c]
        q = q_ref[r0:r0 + sub, hd * V_DIM:(hd + 1) * V_DIM]
        sel = (lane < HEAD_DIM) if comp == 0 else (lane >= HEAD_DIM)
        qc = jnp.where(sel, q, jnp.zeros_like(q))
        scores[c] = [lax.dot_general(qc, k, _NT, preferred_element_type=F32) for k, _ in keys[hd]]

    def stage_softmax(c):
        s = scores.pop(c)
        m = functools.reduce(jnp.maximum, [jnp.max(x, axis=-1, keepdims=True) for x in s])
        p = [jnp.exp2(x - m) for x in s]
        l = functools.reduce(jnp.add, [jnp.sum(x, axis=-1, keepdims=True) for x in p])
        probs[c] = ([x.astype(BF16) for x in p], l)

    def stage_values(c):
        hd, r0, comp = chains[c]
        p, l = probs.pop(c)
        o = functools.reduce(jnp.add, [_dot(x, v) for x, (_, v) in zip(p, keys[hd])])
        outs[c] = o * (1.0 / l)
        if comp == 1:
            o = outs.pop(c - 1) - lam * outs.pop(c)
            o_ref[r0:r0 + sub, hd * V_DIM:(hd + 1) * V_DIM] = _rms(o, sg, SUBLN_EPS).astype(o_ref.dtype)

    for t in range(len(chains) + 2 * lag):
        if t < len(chains):
            stage_scores(t)
        if 0 <= t - lag < len(chains):
            stage_softmax(t - lag)
        if 0 <= t - 2 * lag < len(chains):
            stage_values(t - 2 * lag)


def _attention(lam_params, subln_g, q, k, v, batch, seq_len, cache_k=None, cache_v=None, *,
               tq, heads, sub, lag):
    nq = seq_len // tq
    has_cache = cache_k is not None
    width = heads * V_DIM
    k3 = k.reshape(batch, seq_len, D_MODEL)
    v3 = v.reshape(batch, seq_len, D_MODEL)
    q_spec = pl.BlockSpec((tq, width), lambda b, h, i: (b * nq + i, h))
    kv_spec = pl.BlockSpec((1, seq_len, width), lambda b, h, i: (b, 0, h))
    in_specs = [_const_spec((4, HEAD_DIM)), _const_spec((1, V_DIM)), q_spec, kv_spec, kv_spec]
    args = [lam_params, subln_g.reshape(1, V_DIM), q, k3, v3]
    if has_cache:
        past = cache_k.shape[1]
        c_spec = pl.BlockSpec((1, past, width), lambda b, h, i: (b, 0, h))
        in_specs += [c_spec, c_spec]
        args += [cache_k.reshape(batch, past, D_MODEL), cache_v.reshape(batch, past, D_MODEL)]
    return pl.pallas_call(
        functools.partial(_attn_kernel, has_cache=has_cache, heads=heads, sub=sub, lag=lag),
        grid=(batch, N_HEADS // heads, nq),
        in_specs=in_specs,
        out_specs=q_spec,
        out_shape=jax.ShapeDtypeStruct((batch * seq_len, D_MODEL), BF16),
        compiler_params=_cparams(3),
        name="diff_attn",
    )(*args)


def _swiglu(h, wgu, wd):
    out = None
    for c0, c1 in zip(FF_SPLITS[:-1], FF_SPLITS[1:]):
        g = _dot(h, wgu[:, c0:c1])
        u = _dot(h, wgu[:, D_FF + c0:D_FF + c1])
        down = _dot((g * jax.nn.sigmoid(g) * u).astype(BF16), wd[c0:c1, :])
        out = down if out is None else out + down
    return out


def _dft_tables(n):
    j = np.arange(n, dtype=np.int64)
    ang = (2.0 * np.pi / n) * ((j[:, None] * j[None, :]) % n).astype(np.float64)
    s = n ** -0.5
    return np.cos(ang) * s, np.sin(ang) * s


def _post_attn_kernel(x_ref, o_ref, mod0_ref, mod1_ref, g2_ref, g1n_ref, wo_ref, wgu_ref, wd_ref,
                      cs_ref, out_ref, a_ref, b_ref):
    x1 = x_ref[...] + mod0_ref[0, 2:3, :] * _dot(o_ref[...], wo_ref[...])
    h = _rms(x1, g2_ref[...], EPS) * (1.0 + mod0_ref[0, 4:5, :]) + mod0_ref[0, 3:4, :]
    x2 = x1 + mod0_ref[0, 5:6, :] * _swiglu(h.astype(BF16), wgu_ref, wd_ref)
    out_ref[...] = x2
    h1 = _rms(x2, g1n_ref[...], EPS) * (1.0 + mod1_ref[0, 1:2, :]) + mod1_ref[0, 0:1, :]
    hb = h1.astype(BF16)
    fg = FOURIER_GROUP
    for grp in range(N_FOURIER_GROUPS):
        ab = _dot(hb[:, grp * fg:(grp + 1) * fg], cs_ref[...])
        a_ref[:, grp * fg:(grp + 1) * fg] = ab[:, :fg].astype(BF16)
        b_ref[:, grp * fg:(grp + 1) * fg] = ab[:, fg:].astype(BF16)


def _post_attn(x2d, o, mod0, mod1, g2, g1n, wo, wgu, wd, seq_len, row0):
    t = x2d.shape[0]
    tm = FFN_TILE if row0 == 0 else min(FFN_TILE, seq_len)
    cd, sd = _dft_tables(FOURIER_GROUP)
    cs = jnp.asarray(np.concatenate([cd, sd], axis=1), dtype=F32).astype(BF16)
    row_spec = pl.BlockSpec((tm, D_MODEL), lambda i: (i, 0))
    mod_spec = _mod_spec(seq_len // tm, row0)
    return pl.pallas_call(
        _post_attn_kernel,
        grid=(t // tm,),
        in_specs=[row_spec, row_spec, mod_spec, mod_spec, _const_spec((1, D_MODEL)),
                  _const_spec((1, D_MODEL)), _const_spec((D_MODEL, D_MODEL)),
                  _const_spec((D_MODEL, 2 * D_FF)), _const_spec((D_FF, D_MODEL)),
                  _const_spec((FOURIER_GROUP, 2 * FOURIER_GROUP))],
        out_specs=[row_spec] * 3,
        out_shape=[jax.ShapeDtypeStruct((t, D_MODEL), F32),
                   jax.ShapeDtypeStruct((t, D_MODEL), BF16),
                   jax.ShapeDtypeStruct((t, D_MODEL), BF16)],
        compiler_params=_cparams(1),
        name="post_attn_swiglu",
    )(x2d, o, mod0, mod1, g2.reshape(1, D_MODEL), g1n.reshape(1, D_MODEL), wo, wgu, wd, cs)


def _route(x, mod_ref, g_ref, wr_ref, h_ref, route_ref, cnt_ref):
    tb = MOE_BLOCK
    h = _rms(x, g_ref[...], EPS) * (1.0 + mod_ref[0, 4:5, :]) + mod_ref[0, 3:4, :]
    h_hi, h_lo = _split_bf16(h)
    h_ref[...] = h_hi
    w_hi, w_lo = _split_bf16(wr_ref[...])
    dg = lambda a, b: lax.dot_general(a, b, _NT, preferred_element_type=F32)
    logits = dg(w_hi, h_hi) + dg(w_lo, h_hi) + dg(w_hi, h_lo)
    e = jnp.exp(logits - jnp.max(logits, axis=0, keepdims=True))
    probs = e / jnp.sum(e, axis=0, keepdims=True)
    eidx = lax.broadcasted_iota(jnp.int32, (N_EXPERTS, tb), 0).astype(F32)
    big = float(N_EXPERTS)
    p1 = jnp.max(probs, axis=0, keepdims=True)
    i1 = jnp.min(jnp.where(probs == p1, eidx, big), axis=0, keepdims=True)
    oh1 = eidx == i1
    rest = jnp.where(oh1, -1.0, probs)
    p2 = jnp.max(rest, axis=0, keepdims=True)
    i2 = jnp.min(jnp.where(rest == p2, eidx, big), axis=0, keepdims=True)
    oh2 = eidx == i2
    den = p1 + p2
    oh = jnp.where(oh1 | oh2, 1.0, 0.0)
    before = (lax.broadcasted_iota(jnp.int32, (tb, tb), 0)
              < lax.broadcasted_iota(jnp.int32, (tb, tb), 1))
    rank = _dot(oh.astype(BF16), jnp.where(before, 1.0, 0.0).astype(BF16))
    cnt = jnp.sum(oh, axis=1, keepdims=True)
    cnt_ref[0] = jnp.broadcast_to(cnt, (N_EXPERTS, 128))
    n16 = jnp.floor((cnt + (BF16_SUBLANES - 1.0)) * (1.0 / BF16_SUBLANES)) * BF16_SUBLANES
    ecol = lax.broadcasted_iota(jnp.int32, (N_EXPERTS, 1), 0)
    seg = jnp.zeros_like(n16)
    for ex in range(N_EXPERTS - 1):
        seg = seg + jnp.where(ecol > ex, n16[ex:ex + 1, :], 0.0)
    pos = rank + seg
    r1 = jnp.sum(jnp.where(oh1, pos, 0.0), axis=0, keepdims=True)
    r2 = jnp.sum(jnp.where(oh2, pos, 0.0), axis=0, keepdims=True)
    zero = jnp.zeros_like(r1)
    route_ref[...] = jnp.concatenate([i1, i2, p1 / den, p2 / den, r1, r2, zero, zero], axis=0)


def _fourier_router_kernel(x_ref, mod_ref, cl_ref, sl_ref, a_ref, b_ref, wf_ref, g_ref, wr_ref,
                           out_ref, h_ref, route_ref, cnt_ref):
    ys = [_dot(cl_ref[...], a_ref[j]) + _dot(sl_ref[...], b_ref[j]) for j in range(a_ref.shape[0])]
    y = ys[0] if len(ys) == 1 else jnp.concatenate(ys, axis=0)
    x = x_ref[...] + mod_ref[0, 2:3, :] * _dot(y.astype(BF16), wf_ref[...])
    out_ref[...] = x
    _route(x, mod_ref, g_ref, wr_ref, h_ref, route_ref, cnt_ref)


def _fourier_router(x2d, a, b, mod, wf, g, wr_t, batch, seq_len, row0):
    t = x2d.shape[0]
    rows = MOE_BLOCK
    part = min(seq_len, rows)
    nbat = rows // part
    nr = seq_len // part
    cl, sl = _dft_tables(seq_len)
    cl = jnp.asarray(cl, dtype=F32).astype(BF16)
    sl = jnp.asarray(-sl, dtype=F32).astype(BF16)
    row_spec = pl.BlockSpec((rows, D_MODEL), lambda i: (i, 0))
    tab_spec = pl.BlockSpec((part, seq_len), lambda i: (i % nr, 0))
    ab_spec = pl.BlockSpec((nbat, seq_len, D_MODEL), lambda i: (i // nr, 0, 0))
    nb = t // rows
    return pl.pallas_call(
        _fourier_router_kernel,
        grid=(nb,),
        in_specs=[row_spec, _mod_spec(nr, row0), tab_spec, tab_spec, ab_spec, ab_spec,
                  _const_spec((D_MODEL, D_MODEL)), _const_spec((1, D_MODEL)),
                  _const_spec((N_EXPERTS, D_MODEL))],
        out_specs=[row_spec, row_spec, pl.BlockSpec((8, rows), lambda i: (0, i)),
                   pl.BlockSpec((1, N_EXPERTS, 128), lambda i: (i, 0, 0))],
        out_shape=[jax.ShapeDtypeStruct((t, D_MODEL), F32),
                   jax.ShapeDtypeStruct((t, D_MODEL), BF16),
                   jax.ShapeDtypeStruct((8, t), F32),
                   jax.ShapeDtypeStruct((nb, N_EXPERTS, 128), F32)],
        compiler_params=_cparams(1),
        name="fourier_router",
    )(x2d, mod, cl, sl, a.reshape(batch, seq_len, D_MODEL), b.reshape(batch, seq_len, D_MODEL),
      wf, g.reshape(1, D_MODEL), wr_t)


def _segment_copies(n16, src_row, dst_row, make_copy):
    for bit in SEG_BITS:
        done = n16 & ~(2 * bit - 1)

        @pl.when((n16 & bit) != 0)
        def _():
            make_copy(pl.multiple_of(src_row + done, BF16_SUBLANES),
                      pl.multiple_of(dst_row + done, BF16_SUBLANES), bit)


def _dispatch_kernel(seg_ref, dst_ref, n16_ref, pad_dst_ref, pad_n_ref, nt_ref, *refs,
                     group_blocks, min_tiles):
    n_in = 2 * len(group_blocks)
    xs_ref, comp_ref, zero_ref, sem = refs[n_in:]
    i = pl.program_id(0)
    last = pl.num_programs(0) - 1
    slot = i % 2
    base = i * N_EXPERTS

    def block_copies(base_, slot_, act):
        for e in range(N_EXPERTS):
            _segment_copies(
                n16_ref[base_ + e], seg_ref[base_ + e], dst_ref[base_ + e],
                lambda s, d, n: act(pltpu.make_async_copy(
                    comp_ref.at[slot_, pl.ds(s, n)], xs_ref.at[pl.ds(d, n)], sem.at[slot_])))

    def zero_copies(act):
        for e in range(N_EXPERTS):
            _segment_copies(
                pad_n_ref[e], 0, pad_dst_ref[e],
                lambda s, d, n: act(pltpu.make_async_copy(
                    zero_ref.at[pl.ds(s, n)], xs_ref.at[pl.ds(d, n)], sem.at[2])))
        for tile in range(min_tiles, xs_ref.shape[0] // EXPERT_TILE):
            @pl.when(tile >= nt_ref[0])
            def _():
                act(pltpu.make_async_copy(
                    zero_ref, xs_ref.at[pl.ds(tile * EXPERT_TILE, EXPERT_TILE)], sem.at[2]))

    @pl.when(i == 0)
    def _():
        zero_ref[...] = jnp.zeros_like(zero_ref)
        zero_copies(lambda c: c.start())

    first = 0
    for grp, nb in enumerate(group_blocks):
        h_ref, route_ref = refs[2 * grp], refs[2 * grp + 1]

        @pl.when((i >= first) & (i < first + nb))
        def _():
            r = route_ref[...]
            pos1, pos2 = r[4:5], r[5:6]
            rows = lax.broadcasted_iota(jnp.int32, (MOE_ROWS, MOE_BLOCK), 0).astype(F32)
            onehot = jnp.where((rows == pos1) | (rows == pos2), 1.0, 0.0).astype(BF16)
            comp_ref[slot] = _dot(onehot, h_ref[...]).astype(BF16)
        first += nb
    block_copies(base, slot, lambda c: c.start())

    @pl.when(i > 0)
    def _():
        block_copies(base - N_EXPERTS, 1 - slot, lambda c: c.wait())

    @pl.when(i == last)
    def _():
        block_copies(base, slot, lambda c: c.wait())
        zero_copies(lambda c: c.wait())


def _dispatch(tables, hs, routes, n_rows, min_tiles):
    group_blocks = tuple(h.shape[0] // MOE_BLOCK for h in hs)
    in_specs, args, first = [], [], 0
    for h, route, nb in zip(hs, routes, group_blocks):
        blk = lambda i, first=first, nb=nb: jnp.clip(i - first, 0, nb - 1)
        in_specs += [pl.BlockSpec((MOE_BLOCK, D_MODEL), lambda i, *_, blk=blk: (blk(i), 0)),
                     pl.BlockSpec((8, MOE_BLOCK), lambda i, *_, blk=blk: (0, blk(i)))]
        args += [h, route]
        first += nb
    grid_spec = pltpu.PrefetchScalarGridSpec(
        num_scalar_prefetch=len(tables),
        grid=(sum(group_blocks),),
        in_specs=in_specs,
        out_specs=pl.BlockSpec(memory_space=pl.ANY),
        scratch_shapes=[pltpu.VMEM((2, MOE_ROWS, D_MODEL), BF16),
                        pltpu.VMEM((EXPERT_TILE, D_MODEL), BF16),
                        pltpu.SemaphoreType.DMA((3,))],
    )
    return pl.pallas_call(
        functools.partial(_dispatch_kernel, group_blocks=group_blocks, min_tiles=min_tiles),
        grid_spec=grid_spec,
        out_shape=jax.ShapeDtypeStruct((n_rows, D_MODEL), BF16),
        compiler_params=_cparams(1),
        name="moe_dispatch",
    )(*tables, *args)


def _expert_kernel(te_ref, nt_ref, catch_ref, bge_ref, bgc_ref, xs_ref, wgu_hbm, wd_hbm, ys_ref,
                   wgu_bf, wd_bf, stage_gu, stage_d, sem):
    i = pl.program_id(0)
    used = i < nt_ref[0]
    rg, rd = D_MODEL // W_CHUNKS, D_FF // W_CHUNKS

    def chunk_copies(expert, c, slot):
        return (pltpu.make_async_copy(wgu_hbm.at[expert, pl.ds(pl.multiple_of(c * rg, rg), rg)],
                                      stage_gu.at[slot], sem.at[0, slot]),
                pltpu.make_async_copy(wd_hbm.at[expert, pl.ds(pl.multiple_of(c * rd, rd), rd)],
                                      stage_d.at[slot], sem.at[1, slot]))

    def cast_chunk(expert, c, slot):
        ws = expert % 2
        wgu_bf[ws, pl.ds(pl.multiple_of(c * rg, rg), rg), :] = stage_gu[slot].astype(BF16)
        wd_bf[ws, pl.ds(pl.multiple_of(c * rd, rd), rd), :] = stage_d[slot].astype(BF16)

    @pl.when(used)
    def _():
        e = te_ref[i]

        @pl.when(i == 0)
        def _():
            for cp in chunk_copies(bge_ref[0], bgc_ref[0], 0):
                cp.start()

        @pl.when(i + 1 < nt_ref[0])
        def _():
            for cp in chunk_copies(bge_ref[i + 1], bgc_ref[i + 1], (i + 1) % 2):
                cp.start()

        def catch_up(c, carry):
            cps = chunk_copies(e, c, 2)
            for cp in cps:
                cp.start()
            for cp in cps:
                cp.wait()
            cast_chunk(e, c, 2)
            return carry

        lax.fori_loop(catch_ref[i], W_CHUNKS, catch_up, 0)

        for cp in chunk_copies(bge_ref[i], bgc_ref[i], i % 2):
            cp.wait()
        cast_chunk(bge_ref[i], bgc_ref[i], i % 2)
        ws = e % 2
        ys_ref[...] = _swiglu(xs_ref[...], wgu_bf.at[ws], wd_bf.at[ws]).astype(ys_ref.dtype)

    @pl.when(jnp.logical_not(used))
    def _():
        ys_ref[...] = jnp.zeros_like(ys_ref)


def _experts(tile_tables, xs, wgu_e, wd_e):
    rows = xs.shape[0]
    tm = EXPERT_TILE
    rg, rd = D_MODEL // W_CHUNKS, D_FF // W_CHUNKS
    grid_spec = pltpu.PrefetchScalarGridSpec(
        num_scalar_prefetch=len(tile_tables),
        grid=(rows // tm,),
        in_specs=[pl.BlockSpec((tm, D_MODEL), lambda i, te, nt, *_: (jnp.minimum(i, nt[0] - 1), 0)),
                  pl.BlockSpec(memory_space=pl.ANY), pl.BlockSpec(memory_space=pl.ANY)],
        out_specs=pl.BlockSpec((tm, D_MODEL), lambda i, *_: (i, 0)),
        scratch_shapes=[pltpu.VMEM((2, D_MODEL, 2 * D_FF), BF16), pltpu.VMEM((2, D_FF, D_MODEL), BF16),
                        pltpu.VMEM((3, rg, 2 * D_FF), F32), pltpu.VMEM((3, rd, D_MODEL), F32),
                        pltpu.SemaphoreType.DMA((2, 3))],
    )
    return pl.pallas_call(
        _expert_kernel,
        grid_spec=grid_spec,
        out_shape=jax.ShapeDtypeStruct((rows, D_MODEL), BF16),
        compiler_params=_cparams(1),
        name="moe_experts",
    )(*tile_tables, xs, wgu_e, wd_e)


def _combine_kernel(seg_ref, dst_ref, n16_ref, x_ref, rt_ref, mod_ref, g_ref, ys_ref, out_ref,
                    buf_ref, sem, *, block0):
    i = pl.program_id(0)
    slot = i % 2
    base = (block0 + i) * N_EXPERTS

    def block_copies(base_, slot_, act):
        for e in range(N_EXPERTS):
            _segment_copies(
                n16_ref[base_ + e], seg_ref[base_ + e], dst_ref[base_ + e],
                lambda s, d, n: act(pltpu.make_async_copy(
                    ys_ref.at[pl.ds(d, n)], buf_ref.at[slot_, pl.ds(s, n)], sem.at[slot_])))

    def fetch(base_, slot_):
        buf_ref[slot_] = jnp.zeros(buf_ref.shape[1:], buf_ref.dtype)
        block_copies(base_, slot_, lambda c: c.start())

    @pl.when(i == 0)
    def _():
        fetch(base, slot)

    @pl.when(i + 1 < pl.num_programs(0))
    def _():
        fetch(base + N_EXPERTS, 1 - slot)

    rt = rt_ref[...]
    pos1, pos2 = rt[:, 4:5], rt[:, 5:6]
    cols = lax.broadcasted_iota(jnp.int32, (MOE_BLOCK, MOE_ROWS), 1).astype(F32)
    gates = (jnp.where(cols == pos1, rt[:, 2:3], 0.0)
             + jnp.where(cols == pos2, rt[:, 3:4], 0.0)).astype(BF16)
    block_copies(base, slot, lambda c: c.wait())
    x = x_ref[...] + mod_ref[0, 5:6, :] * _dot(gates, buf_ref[slot])
    out_ref[...] = _rms(x, g_ref[...], EPS)


def _combine(tables, x2d, route_t, mod, g, ys, seq_len, row0, block0):
    t = x2d.shape[0]
    tb = MOE_BLOCK
    grid_spec = pltpu.PrefetchScalarGridSpec(
        num_scalar_prefetch=len(tables),
        grid=(t // tb,),
        in_specs=[pl.BlockSpec((tb, D_MODEL), lambda i, *_: (i, 0)),
                  pl.BlockSpec((tb, 8), lambda i, *_: (i, 0)),
                  _mod_spec(max(seq_len // tb, 1), row0),
                  pl.BlockSpec((1, D_MODEL), lambda i, *_: (0, 0)),
                  pl.BlockSpec(memory_space=pl.ANY)],
        out_specs=pl.BlockSpec((tb, D_MODEL), lambda i, *_: (i, 0)),
        scratch_shapes=[pltpu.VMEM((2, MOE_ROWS, D_MODEL), BF16), pltpu.SemaphoreType.DMA((2,))],
    )
    return pl.pallas_call(
        functools.partial(_combine_kernel, block0=block0),
        grid_spec=grid_spec,
        out_shape=jax.ShapeDtypeStruct((t, D_MODEL), F32),
        compiler_params=_cparams(1),
        name="moe_combine",
    )(*tables, x2d, route_t, mod, g.reshape(1, D_MODEL), ys)


def _moe_tables(counts, n_rows):
    pad = BF16_SUBLANES
    n16 = (counts + pad - 1) // pad * pad
    seg = jnp.cumsum(n16, axis=1) - n16
    total = jnp.sum(n16, axis=0)
    region = (total + EXPERT_TILE - 1) // EXPERT_TILE * EXPERT_TILE
    region_end = jnp.cumsum(region)
    region_start = region_end - region
    dst = region_start[None, :] + jnp.cumsum(n16, axis=0) - n16
    tiles_end = region_end // EXPERT_TILE
    tile_ids = jnp.arange(n_rows // EXPERT_TILE, dtype=jnp.int32)
    tile_expert = jnp.minimum(jnp.sum(tile_ids[:, None] >= tiles_end[None, :], axis=1), N_EXPERTS - 1)
    n_tiles_e = region // EXPERT_TILE
    local = tile_ids - (tiles_end - n_tiles_e)[tile_expert]
    prev_tiles = jnp.concatenate([jnp.zeros((1,), n_tiles_e.dtype), n_tiles_e[:-1]])
    catch_from = jnp.where(local == 0, jnp.minimum(prev_tiles[tile_expert], W_CHUNKS), W_CHUNKS)
    ahead_expert = jnp.where(tile_expert < N_EXPERTS - 1, tile_expert + 1, N_EXPERTS - 2)
    ahead_chunk = jnp.clip(local, 0, W_CHUNKS - 1)
    flat = lambda a: a.reshape(-1).astype(jnp.int32)
    block_tables = (flat(seg), flat(dst), flat(n16))
    pad_tables = (flat(region_start + total), flat(region - total))
    tile_tables = (flat(tile_expert), flat(tiles_end[-1:]), flat(catch_from), flat(ahead_expert),
                   flat(ahead_chunk))
    return block_tables, pad_tables, tile_tables


def kernel(x_prompt, x_sample, c, cache_k_0, cache_v_0, c_ctx, ada_w_0, ada_b_0, norm1_g_0, norm2_g_0, w_qkv_0, lambda_q1_0, lambda_k1_0, lambda_q2_0, lambda_k2_0, subln_g_0, w_o_0, w_gu_0, w_down_0, ada_w_1, ada_b_1, norm1_g_1, norm2_g_1, w_fourier_1, w_router_1, w_gu_e_1, w_down_e_1, final_norm_g):
    bp, lp, _ = x_prompt.shape
    bs, ls, _ = x_sample.shape
    assert 1 + bs <= ADA_ROWS and (bp * lp) % MOE_BLOCK == 0 and ls % MOE_BLOCK == 0
    assert MOE_BLOCK % lp == 0 or lp % MOE_BLOCK == 0

    cond = jnp.zeros((ADA_ROWS, D_MODEL), F32).at[0].set(c_ctx).at[1:1 + bs].set(c)
    mod0 = _adaln(cond, ada_w_0, ada_b_0)
    mod1 = _adaln(cond, ada_w_1, ada_b_1)
    lam_params = jnp.stack([lambda_q1_0, lambda_k1_0, lambda_q2_0, lambda_k2_0])

    w_qkv = w_qkv_0.astype(BF16)
    w_o = w_o_0.astype(BF16)
    w_gu = w_gu_0.astype(BF16)
    w_down = w_down_0.astype(BF16)
    w_f = w_fourier_1.astype(BF16)
    w_router_t = w_router_1.T

    groups = [dict(x=x_prompt.reshape(bp * lp, D_MODEL), batch=bp, seq=lp, row0=0, rope=False),
              dict(x=x_sample.reshape(bs * ls, D_MODEL), batch=bs, seq=ls, row0=1, rope=True)]

    k_ctx = v_ctx = None
    for gr in groups:
        x, batch, seq, row0 = gr["x"], gr["batch"], gr["seq"], gr["row0"]
        q, k, v = _qkv(x, mod0, norm1_g_0, w_qkv, seq, row0, gr["rope"], BF16 if gr["rope"] else F32)
        if gr["rope"]:
            o = _attention(lam_params, subln_g_0, q, k, v, batch, seq, cache_k_0, cache_v_0,
                           tq=SAMPLE_Q_TILE, heads=SAMPLE_HEADS, sub=SAMPLE_Q_SUB, lag=ATTN_LAG)
        else:
            k_ctx, v_ctx = k, v
            o = _attention(lam_params, subln_g_0, q, k, v, batch, seq, tq=seq, heads=N_HEADS, sub=seq,
                           lag=ATTN_LAG)
        x, fa, fb = _post_attn(x, o, mod0, mod1, norm2_g_0, norm1_g_1, w_o, w_gu, w_down, seq, row0)
        gr["x"], gr["h"], gr["route"], gr["cnt"] = _fourier_router(
            x, fa, fb, mod1, w_f, norm2_g_1, w_router_t, batch, seq, row0)

    n_blocks = [gr["x"].shape[0] // MOE_BLOCK for gr in groups]
    n_pairs = 2 * sum(gr["x"].shape[0] for gr in groups)
    max_rows = n_pairs + sum(n_blocks) * N_EXPERTS * (BF16_SUBLANES - 1) + N_EXPERTS * EXPERT_TILE
    max_rows = (max_rows + EXPERT_TILE - 1) // EXPERT_TILE * EXPERT_TILE
    counts = jnp.concatenate([gr["cnt"][:, :, 0] for gr in groups], axis=0).astype(jnp.int32)
    block_tables, pad_tables, tile_tables = _moe_tables(counts, max_rows)

    xs = _dispatch(block_tables + pad_tables + tile_tables[1:2], [gr["h"] for gr in groups],
                   [gr["route"] for gr in groups], max_rows, n_pairs // EXPERT_TILE)
    ys = _experts(tile_tables, xs, w_gu_e_1, w_down_e_1)
    outs = []
    block0 = 0
    for gr, nb in zip(groups, n_blocks):
        outs.append(_combine(block_tables, gr["x"], gr["route"].T, mod1, final_norm_g, ys,
                             gr["seq"], gr["row0"], block0))
        block0 += nb

    y_prompt = outs[0].reshape(bp, lp, D_MODEL)
    y_sample = outs[1].reshape(bs, ls, D_MODEL)
    return (y_prompt, y_sample,
            k_ctx.reshape(bp, lp, N_HEADS, 2 * HEAD_DIM), v_ctx.reshape(bp, lp, N_HEADS, V_DIM))
```

```python
import functools
import math

import jax
import jax.numpy as jnp
import numpy as np
from jax import lax
from jax.experimental import pallas as pl
from jax.experimental.pallas import tpu as pltpu

F32 = jnp.float32
BF16 = jnp.bfloat16

D_MODEL = 1024
N_HEADS = 8
HEAD_DIM = 64
V_DIM = 2 * HEAD_DIM
GRID_W = 64
AXIS_DIM = HEAD_DIM // 2
ROPE_THETA = 10000.0
N_FOURIER_GROUPS = 4
FOURIER_GROUP = D_MODEL // N_FOURIER_GROUPS
D_FF = 2816
N_EXPERTS = 8
N_MOD = 6
EPS = 1e-6
SUBLN_EPS = 1e-5
LAMBDA_INIT_0 = 0.8 - 0.6 * math.exp(-0.3 * 0)
Q_SCALE = HEAD_DIM ** -0.5 * math.log2(math.e)

V7X_VMEM_BYTES = 64 * 1024 * 1024
VMEM_LIMIT = V7X_VMEM_BYTES - 8 * 1024 * 1024
V7X_MXU_DIM = 256
BF16_SUBLANES = 16

ADA_ROWS = 16
ADA_TN = 1536
QKV_TILE = 512
FFN_TILE = 512
FF_SPLITS = (0, 6 * V7X_MXU_DIM, D_FF)
SAMPLE_Q_TILE = 2048
SAMPLE_HEADS = 1
SAMPLE_Q_SUB = 128
ATTN_LAG = 1
MOE_BLOCK = 512
MOE_ROWS = 2 * MOE_BLOCK + 128
EXPERT_TILE = 256
W_CHUNKS = 16
SEG_BITS = (512, 256, 128, 64, 32, 16)

_NT = (((1,), (1,)), ((), ()))


def _dot(a, b):
    return jnp.dot(a, b, preferred_element_type=F32)


def _split_bf16(x):
    hi = x.astype(BF16)
    lo = (x - hi.astype(F32)).astype(BF16)
    return hi, lo


def _rms(x, g, eps):
    return x * lax.rsqrt(jnp.mean(x * x, axis=-1, keepdims=True) + eps) * g


def _cparams(n_grid, vmem=VMEM_LIMIT):
    return pltpu.CompilerParams(dimension_semantics=("arbitrary",) * n_grid, vmem_limit_bytes=vmem)


def _const_spec(shape):
    nd = len(shape)
    return pl.BlockSpec(shape, lambda *_: (0,) * nd, pipeline_mode=pl.Buffered(1))


def _adaln_kernel(c_ref, w_ref, b_ref, o_ref):
    c = c_ref[...]
    a_hi, a_lo = _split_bf16(c * jax.nn.sigmoid(c))
    w_hi, w_lo = _split_bf16(w_ref[...])
    o_ref[...] = _dot(a_hi, w_hi) + _dot(a_hi, w_lo) + _dot(a_lo, w_hi) + b_ref[...]


def _adaln(cond, w, b):
    n = N_MOD * D_MODEL
    out = pl.pallas_call(
        _adaln_kernel,
        grid=(n // ADA_TN,),
        in_specs=[pl.BlockSpec((ADA_ROWS, D_MODEL), lambda j: (0, 0)),
                  pl.BlockSpec((D_MODEL, ADA_TN), lambda j: (0, j)),
                  pl.BlockSpec((1, ADA_TN), lambda j: (0, j))],
        out_specs=pl.BlockSpec((ADA_ROWS, ADA_TN), lambda j: (0, j)),
        out_shape=jax.ShapeDtypeStruct((ADA_ROWS, n), F32),
        compiler_params=_cparams(1),
        name="adaln",
    )(cond, w, b.reshape(1, n))
    return out.reshape(ADA_ROWS, N_MOD, D_MODEL)


def _mod_spec(seq_tiles, row0):
    if row0 == 0:
        return pl.BlockSpec((1, N_MOD, D_MODEL), lambda i, *_: (0, 0, 0))
    return pl.BlockSpec((1, N_MOD, D_MODEL), lambda i, *_: (row0 + i // seq_tiles, 0, 0))


def _qkv_kernel(*refs, rope):
    if rope:
        x_ref, mod_ref, g_ref, w_ref, cos_ref, sa_ref, sb_ref, q_ref, k_ref, v_ref = refs
    else:
        x_ref, mod_ref, g_ref, w_ref, q_ref, k_ref, v_ref = refs
    h = _rms(x_ref[...], g_ref[...], EPS)
    h = h * (1.0 + mod_ref[0, 1:2, :]) + mod_ref[0, 0:1, :]
    qkv = _dot(h.astype(BF16), w_ref[...])
    inner = N_HEADS * 2 * HEAD_DIM
    for which, ref in ((0, q_ref), (1, k_ref), (2, v_ref)):
        part = qkv[:, which * inner:(which + 1) * inner]
        if rope and which < 2:
            cos, sa, sb = cos_ref[...], sa_ref[...], sb_ref[...]
            for hd in range(N_HEADS):
                blk = part[:, hd * V_DIM:(hd + 1) * V_DIM]
                blk = (blk * cos + pltpu.roll(blk, V_DIM - AXIS_DIM // 2, 1) * sa
                       + pltpu.roll(blk, AXIS_DIM // 2, 1) * sb)
                if which == 0:
                    blk = blk * Q_SCALE
                ref[:, hd * V_DIM:(hd + 1) * V_DIM] = blk.astype(ref.dtype)
        else:
            if which == 0:
                part = part * Q_SCALE
            ref[...] = part.astype(ref.dtype)


def _rope_tables(length):
    rows = length // GRID_W
    row = jnp.repeat(jnp.arange(rows), GRID_W).astype(F32)
    col = jnp.tile(jnp.arange(GRID_W), rows).astype(F32)
    inv = 1.0 / (ROPE_THETA ** (jnp.arange(0, AXIS_DIM, 2, dtype=F32) / AXIS_DIM))
    ar = row[:, None] * inv[None, :]
    ac = col[:, None] * inv[None, :]
    ang = jnp.concatenate([ar, ar, ac, ac], axis=-1)
    cos, sin = jnp.cos(ang), jnp.sin(ang)
    first = (jnp.arange(HEAD_DIM) % AXIS_DIM) < (AXIS_DIM // 2)
    sin_a = jnp.where(first[None, :], -sin, 0.0)
    sin_b = jnp.where(first[None, :], 0.0, sin)
    wide = lambda t: jnp.concatenate([t, t], axis=-1)
    return wide(cos), wide(sin_a), wide(sin_b)


def _qkv(x2d, mod, g, w_bf16, seq_len, row0, rope, kv_dtype):
    t = x2d.shape[0]
    tm = QKV_TILE
    seq_tiles = seq_len // tm
    row_spec = pl.BlockSpec((tm, D_MODEL), lambda i: (i, 0))
    in_specs = [row_spec, _mod_spec(seq_tiles, row0), _const_spec((1, D_MODEL)),
                _const_spec((D_MODEL, 3 * D_MODEL))]
    args = [x2d, mod, g.reshape(1, D_MODEL), w_bf16]
    if rope:
        tab_spec = pl.BlockSpec((tm, V_DIM), lambda i: (i % seq_tiles, 0))
        in_specs += [tab_spec] * 3
        args += list(_rope_tables(seq_len))
    return pl.pallas_call(
        functools.partial(_qkv_kernel, rope=rope),
        grid=(t // tm,),
        in_specs=in_specs,
        out_specs=[row_spec] * 3,
        out_shape=[jax.ShapeDtypeStruct((t, D_MODEL), BF16),
                   jax.ShapeDtypeStruct((t, D_MODEL), kv_dtype),
                   jax.ShapeDtypeStruct((t, D_MODEL), kv_dtype)],
        compiler_params=_cparams(1),
        name="qkv",
    )(*args)


def _attn_kernel(*refs, has_cache, heads, sub, lag):
    if has_cache:
        lam_ref, sg_ref, q_ref, k_ref, v_ref, ck_ref, cv_ref, o_ref = refs
    else:
        lam_ref, sg_ref, q_ref, k_ref, v_ref, o_ref = refs
    lp = lam_ref[...]
    lam = (jnp.exp(jnp.sum(lp[0:1] * lp[1:2], axis=-1, keepdims=True))
           - jnp.exp(jnp.sum(lp[2:3] * lp[3:4], axis=-1, keepdims=True)) + LAMBDA_INIT_0)
    lane = lax.broadcasted_iota(jnp.int32, (1, V_DIM), 1)
    sg = sg_ref[...] * (1.0 - LAMBDA_INIT_0)
    def head_keys(hd):
        cols = slice(hd * V_DIM, (hd + 1) * V_DIM)
        keys = [(k_ref[0, :, cols].astype(BF16), v_ref[0, :, cols].astype(BF16))]
        if has_cache:
            keys.append((ck_ref[0, :, cols].astype(BF16), cv_ref[0, :, cols].astype(BF16)))
        return keys

    chains = [(hd, r0, comp) for hd in range(heads) for r0 in range(0, q_ref.shape[0], sub)
              for comp in range(2)]
    keys = {hd: head_keys(hd) for hd in range(heads)}
    scores, probs, outs = {}, {}, {}

    def stage_scores(c):
        hd, r0, comp = chains[c]
        q = q_ref[r0:r0 + sub, hd * V_DIM:(hd + 1) * V_DIM]
        sel = (lane < HEAD_DIM) if comp == 0 else (lane >= HEAD_DIM)
        qc = jnp.where(sel, q, jnp.zeros_like(q))
        scores[c] = [lax.dot_general(qc, k, _NT, preferred_element_type=F32) for k, _ in keys[hd]]

    def stage_softmax(c):
        s = scores.pop(c)
        m = functools.reduce(jnp.maximum, [jnp.max(x, axis=-1, keepdims=True) for x in s])
        p = [jnp.exp2(x - m) for x in s]
        l = functools.reduce(jnp.add, [jnp.sum(x, axis=-1, keepdims=True) for x in p])
        probs[c] = ([x.astype(BF16) for x in p], l)

    def stage_values(c):
        hd, r0, comp = chains[c]
        p, l = probs.pop(c)
        o = functools.reduce(jnp.add, [_dot(x, v) for x, (_, v) in zip(p, keys[hd])])
        outs[c] = o * (1.0 / l)
        if comp == 1:
            o = outs.pop(c - 1) - lam * outs.pop(c)
            o_ref[r0:r0 + sub, hd * V_DIM:(hd + 1) * V_DIM] = _rms(o, sg, SUBLN_EPS).astype(o_ref.dtype)

    for t in range(len(chains) + 2 * lag):
        if t < len(chains):
            stage_scores(t)
        if 0 <= t - lag < len(chains):
            stage_softmax(t - lag)
        if 0 <= t - 2 * lag < len(chains):
            stage_values(t - 2 * lag)


def _attention(lam_params, subln_g, q, k, v, batch, seq_len, cache_k=None, cache_v=None, *,
               tq, heads, sub, lag):
    nq = seq_len // tq
    has_cache = cache_k is not None
    width = heads * V_DIM
    k3 = k.reshape(batch, seq_len, D_MODEL)
    v3 = v.reshape(batch, seq_len, D_MODEL)
    q_spec = pl.BlockSpec((tq, width), lambda b, h, i: (b * nq + i, h))
    kv_spec = pl.BlockSpec((1, seq_len, width), lambda b, h, i: (b, 0, h))
    in_specs = [_const_spec((4, HEAD_DIM)), _const_spec((1, V_DIM)), q_spec, kv_spec, kv_spec]
    args = [lam_params, subln_g.reshape(1, V_DIM), q, k3, v3]
    if has_cache:
        past = cache_k.shape[1]
        c_spec = pl.BlockSpec((1, past, width), lambda b, h, i: (b, 0, h))
        in_specs += [c_spec, c_spec]
        args += [cache_k.reshape(batch, past, D_MODEL), cache_v.reshape(batch, past, D_MODEL)]
    return pl.pallas_call(
        functools.partial(_attn_kernel, has_cache=has_cache, heads=heads, sub=sub, lag=lag),
        grid=(batch, N_HEADS // heads, nq),
        in_specs=in_specs,
        out_specs=q_spec,
        out_shape=jax.ShapeDtypeStruct((batch * seq_len, D_MODEL), BF16),
        compiler_params=_cparams(3),
        name="diff_attn",
    )(*args)


def _swiglu(h, wgu, wd):
    out = None
    for c0, c1 in zip(FF_SPLITS[:-1], FF_SPLITS[1:]):
        g = _dot(h, wgu[:, c0:c1])
        u = _dot(h, wgu[:, D_FF + c0:D_FF + c1])
        down = _dot((g * jax.nn.sigmoid(g) * u).astype(BF16), wd[c0:c1, :])
        out = down if out is None else out + down
    return out


def _dft_tables(n):
    j = np.arange(n, dtype=np.int64)
    ang = (2.0 * np.pi / n) * ((j[:, None] * j[None, :]) % n).astype(np.float64)
    s = n ** -0.5
    return np.cos(ang) * s, np.sin(ang) * s


def _post_attn_kernel(x_ref, o_ref, mod0_ref, mod1_ref, g2_ref, g1n_ref, wo_ref, wgu_ref, wd_ref,
                      cs_ref, out_ref, a_ref, b_ref):
    x1 = x_ref[...] + mod0_ref[0, 2:3, :] * _dot(o_ref[...], wo_ref[...])
    h = _rms(x1, g2_ref[...], EPS) * (1.0 + mod0_ref[0, 4:5, :]) + mod0_ref[0, 3:4, :]
    x2 = x1 + mod0_ref[0, 5:6, :] * _swiglu(h.astype(BF16), wgu_ref, wd_ref)
    out_ref[...] = x2
    h1 = _rms(x2, g1n_ref[...], EPS) * (1.0 + mod1_ref[0, 1:2, :]) + mod1_ref[0, 0:1, :]
    hb = h1.astype(BF16)
    fg = FOURIER_GROUP
    for grp in range(N_FOURIER_GROUPS):
        ab = _dot(hb[:, grp * fg:(grp + 1) * fg], cs_ref[...])
        a_ref[:, grp * fg:(grp + 1) * fg] = ab[:, :fg].astype(BF16)
        b_ref[:, grp * fg:(grp + 1) * fg] = ab[:, fg:].astype(BF16)


def _post_attn(x2d, o, mod0, mod1, g2, g1n, wo, wgu, wd, seq_len, row0):
    t = x2d.shape[0]
    tm = FFN_TILE if row0 == 0 else min(FFN_TILE, seq_len)
    cd, sd = _dft_tables(FOURIER_GROUP)
    cs = jnp.asarray(np.concatenate([cd, sd], axis=1), dtype=F32).astype(BF16)
    row_spec = pl.BlockSpec((tm, D_MODEL), lambda i: (i, 0))
    mod_spec = _mod_spec(seq_len // tm, row0)
    return pl.pallas_call(
        _post_attn_kernel,
        grid=(t // tm,),
        in_specs=[row_spec, row_spec, mod_spec, mod_spec, _const_spec((1, D_MODEL)),
                  _const_spec((1, D_MODEL)), _const_spec((D_MODEL, D_MODEL)),
                  _const_spec((D_MODEL, 2 * D_FF)), _const_spec((D_FF, D_MODEL)),
                  _const_spec((FOURIER_GROUP, 2 * FOURIER_GROUP))],
        out_specs=[row_spec] * 3,
        out_shape=[jax.ShapeDtypeStruct((t, D_MODEL), F32),
                   jax.ShapeDtypeStruct((t, D_MODEL), BF16),
                   jax.ShapeDtypeStruct((t, D_MODEL), BF16)],
        compiler_params=_cparams(1),
        name="post_attn_swiglu",
    )(x2d, o, mod0, mod1, g2.reshape(1, D_MODEL), g1n.reshape(1, D_MODEL), wo, wgu, wd, cs)


def _route(x, mod_ref, g_ref, wr_ref, h_ref, route_ref, cnt_ref):
    tb = MOE_BLOCK
    h = _rms(x, g_ref[...], EPS) * (1.0 + mod_ref[0, 4:5, :]) + mod_ref[0, 3:4, :]
    h_hi, h_lo = _split_bf16(h)
    h_ref[...] = h_hi
    w_hi, w_lo = _split_bf16(wr_ref[...])
    dg = lambda a, b: lax.dot_general(a, b, _NT, preferred_element_type=F32)
    logits = dg(w_hi, h_hi) + dg(w_lo, h_hi) + dg(w_hi, h_lo)
    e = jnp.exp(logits - jnp.max(logits, axis=0, keepdims=True))
    probs = e / jnp.sum(e, axis=0, keepdims=True)
    eidx = lax.broadcasted_iota(jnp.int32, (N_EXPERTS, tb), 0).astype(F32)
    big = float(N_EXPERTS)
    p1 = jnp.max(probs, axis=0, keepdims=True)
    i1 = jnp.min(jnp.where(probs == p1, eidx, big), axis=0, keepdims=True)
    oh1 = eidx == i1
    rest = jnp.where(oh1, -1.0, probs)
    p2 = jnp.max(rest, axis=0, keepdims=True)
    i2 = jnp.min(jnp.where(rest == p2, eidx, big), axis=0, keepdims=True)
    oh2 = eidx == i2
    den = p1 + p2
    oh = jnp.where(oh1 | oh2, 1.0, 0.0)
    before = (lax.broadcasted_iota(jnp.int32, (tb, tb), 0)
              < lax.broadcasted_iota(jnp.int32, (tb, tb), 1))
    rank = _dot(oh.astype(BF16), jnp.where(before, 1.0, 0.0).astype(BF16))
    cnt = jnp.sum(oh, axis=1, keepdims=True)
    cnt_ref[0] = jnp.broadcast_to(cnt, (N_EXPERTS, 128))
    n16 = jnp.floor((cnt + (BF16_SUBLANES - 1.0)) * (1.0 / BF16_SUBLANES)) * BF16_SUBLANES
    ecol = lax.broadcasted_iota(jnp.int32, (N_EXPERTS, 1), 0)
    seg = jnp.zeros_like(n16)
    for ex in range(N_EXPERTS - 1):
        seg = seg + jnp.where(ecol > ex, n16[ex:ex + 1, :], 0.0)
    pos = rank + seg
    r1 = jnp.sum(jnp.where(oh1, pos, 0.0), axis=0, keepdims=True)
    r2 = jnp.sum(jnp.where(oh2, pos, 0.0), axis=0, keepdims=True)
    zero = jnp.zeros_like(r1)
    route_ref[...] = jnp.concatenate([i1, i2, p1 / den, p2 / den, r1, r2, zero, zero], axis=0)


def _fourier_router_kernel(x_ref, mod_ref, cl_ref, sl_ref, a_ref, b_ref, wf_ref, g_ref, wr_ref,
                           out_ref, h_ref, route_ref, cnt_ref):
    ys = [_dot(cl_ref[...], a_ref[j]) + _dot(sl_ref[...], b_ref[j]) for j in range(a_ref.shape[0])]
    y = ys[0] if len(ys) == 1 else jnp.concatenate(ys, axis=0)
    x = x_ref[...] + mod_ref[0, 2:3, :] * _dot(y.astype(BF16), wf_ref[...])
    out_ref[...] = x
    _route(x, mod_ref, g_ref, wr_ref, h_ref, route_ref.at[0], cnt_ref)


def _fourier_split_router_kernel(x_ref, mod_ref, ce_ref, se_ref, co_ref, so_ref, a_ref, b_ref, wf_ref,
                                 g_ref, wr_ref, out_ref, h_ref, route_ref, cnt_ref):
    half_len = a_ref.shape[2] // 2
    even = _dot(ce_ref[...], a_ref[0, :, :half_len]) + _dot(se_ref[...], b_ref[0, :, :half_len])
    odd = _dot(co_ref[...], a_ref[0, :, half_len:]) + _dot(so_ref[...], b_ref[0, :, half_len:])
    for half, y in ((0, even + odd), (1, even - odd)):
        x = x_ref[0, half] + mod_ref[0, 2:3, :] * _dot(y.astype(BF16), wf_ref[...])
        out_ref[0, half] = x
        _route(x, mod_ref, g_ref, wr_ref, h_ref.at[0, half], route_ref.at[0, half, 0],
               cnt_ref.at[0, half])


def _fourier_split_router(x2d, a, b, mod, wf, g, wr_t, batch, seq_len, row0):
    t = x2d.shape[0]
    rows = MOE_BLOCK
    half_len = seq_len // 2
    nr = half_len // rows
    j = np.arange(half_len, dtype=np.int64)[:, None]
    m = np.arange(half_len, dtype=np.int64)[None, :]
    scale = seq_len ** -0.5

    def table(fn, k, sign):
        ang = (2.0 * np.pi / seq_len) * ((j * k) % seq_len).astype(np.float64)
        return jnp.asarray(sign * scale * fn(ang), dtype=F32).astype(BF16)

    tables = [table(np.cos, 2 * m, 1.0), table(np.sin, 2 * m, -1.0),
              table(np.cos, 2 * m + 1, 1.0), table(np.sin, 2 * m + 1, -1.0)]
    row4 = pl.BlockSpec((1, 2, rows, D_MODEL), lambda i: (i // nr, 0, i % nr, 0))
    tab_spec = pl.BlockSpec((rows, half_len), lambda i: (i % nr, 0))
    ab_spec = pl.BlockSpec((1, half_len, 2 * D_MODEL), lambda i: (i // nr, 0, 0))
    shape4 = (batch, 2, half_len, D_MODEL)
    out, h, route, cnt = pl.pallas_call(
        _fourier_split_router_kernel,
        grid=(batch * nr,),
        in_specs=[row4, _mod_spec(nr, row0)] + [tab_spec] * 4 + [ab_spec, ab_spec,
                  _const_spec((D_MODEL, D_MODEL)), _const_spec((1, D_MODEL)),
                  _const_spec((N_EXPERTS, D_MODEL))],
        out_specs=[row4, row4,
                   pl.BlockSpec((1, 2, 1, 8, rows), lambda i: (i // nr, 0, i % nr, 0, 0)),
                   pl.BlockSpec((1, 2, 1, N_EXPERTS, 128), lambda i: (i // nr, 0, i % nr, 0, 0))],
        out_shape=[jax.ShapeDtypeStruct(shape4, F32), jax.ShapeDtypeStruct(shape4, BF16),
                   jax.ShapeDtypeStruct((batch, 2, nr, 8, rows), F32),
                   jax.ShapeDtypeStruct((batch, 2, nr, N_EXPERTS, 128), F32)],
        compiler_params=_cparams(1),
        name="fourier_split_router",
    )(x2d.reshape(shape4), mod, *tables, a.reshape(batch, half_len, 2 * D_MODEL),
      b.reshape(batch, half_len, 2 * D_MODEL), wf, g.reshape(1, D_MODEL), wr_t)
    nb = t // rows
    return (out.reshape(t, D_MODEL), h.reshape(t, D_MODEL), route.reshape(nb, 8, rows),
            cnt.reshape(nb, N_EXPERTS, 128))


def _fourier_router(x2d, a, b, mod, wf, g, wr_t, batch, seq_len, row0):
    t = x2d.shape[0]
    rows = MOE_BLOCK
    if seq_len % (2 * rows) == 0:
        return _fourier_split_router(x2d, a, b, mod, wf, g, wr_t, batch, seq_len, row0)
    part = min(seq_len, rows)
    nbat = rows // part
    nr = seq_len // part
    cl, sl = _dft_tables(seq_len)
    cl = jnp.asarray(cl, dtype=F32).astype(BF16)
    sl = jnp.asarray(-sl, dtype=F32).astype(BF16)
    row_spec = pl.BlockSpec((rows, D_MODEL), lambda i: (i, 0))
    tab_spec = pl.BlockSpec((part, seq_len), lambda i: (i % nr, 0))
    ab_spec = pl.BlockSpec((nbat, seq_len, D_MODEL), lambda i: (i // nr, 0, 0))
    nb = t // rows
    return pl.pallas_call(
        _fourier_router_kernel,
        grid=(nb,),
        in_specs=[row_spec, _mod_spec(nr, row0), tab_spec, tab_spec, ab_spec, ab_spec,
                  _const_spec((D_MODEL, D_MODEL)), _const_spec((1, D_MODEL)),
                  _const_spec((N_EXPERTS, D_MODEL))],
        out_specs=[row_spec, row_spec, pl.BlockSpec((1, 8, rows), lambda i: (i, 0, 0)),
                   pl.BlockSpec((1, N_EXPERTS, 128), lambda i: (i, 0, 0))],
        out_shape=[jax.ShapeDtypeStruct((t, D_MODEL), F32),
                   jax.ShapeDtypeStruct((t, D_MODEL), BF16),
                   jax.ShapeDtypeStruct((nb, 8, rows), F32),
                   jax.ShapeDtypeStruct((nb, N_EXPERTS, 128), F32)],
        compiler_params=_cparams(1),
        name="fourier_router",
    )(x2d, mod, cl, sl, a.reshape(batch, seq_len, D_MODEL), b.reshape(batch, seq_len, D_MODEL),
      wf, g.reshape(1, D_MODEL), wr_t)


def _segment_copies(n16, src_row, dst_row, make_copy):
    for bit in SEG_BITS:
        done = n16 & ~(2 * bit - 1)

        @pl.when((n16 & bit) != 0)
        def _():
            make_copy(pl.multiple_of(src_row + done, BF16_SUBLANES),
                      pl.multiple_of(dst_row + done, BF16_SUBLANES), bit)


def _dispatch_kernel(seg_ref, dst_ref, n16_ref, pad_dst_ref, pad_n_ref, nt_ref, *refs,
                     group_blocks, min_tiles):
    n_in = 2 * len(group_blocks)
    xs_ref, comp_ref, zero_ref, sem = refs[n_in:]
    i = pl.program_id(0)
    last = pl.num_programs(0) - 1
    slot = i % 2
    base = i * N_EXPERTS

    def block_copies(base_, slot_, act):
        for e in range(N_EXPERTS):
            _segment_copies(
                n16_ref[base_ + e], seg_ref[base_ + e], dst_ref[base_ + e],
                lambda s, d, n: act(pltpu.make_async_copy(
                    comp_ref.at[slot_, pl.ds(s, n)], xs_ref.at[pl.ds(d, n)], sem.at[slot_])))

    def zero_copies(act):
        for e in range(N_EXPERTS):
            _segment_copies(
                pad_n_ref[e], 0, pad_dst_ref[e],
                lambda s, d, n: act(pltpu.make_async_copy(
                    zero_ref.at[pl.ds(s, n)], xs_ref.at[pl.ds(d, n)], sem.at[2])))
        for tile in range(min_tiles, xs_ref.shape[0] // EXPERT_TILE):
            @pl.when(tile >= nt_ref[0])
            def _():
                act(pltpu.make_async_copy(
                    zero_ref, xs_ref.at[pl.ds(tile * EXPERT_TILE, EXPERT_TILE)], sem.at[2]))

    @pl.when(i == 0)
    def _():
        zero_ref[...] = jnp.zeros_like(zero_ref)
        zero_copies(lambda c: c.start())

    first = 0
    for grp, nb in enumerate(group_blocks):
        h_ref, route_ref = refs[2 * grp], refs[2 * grp + 1]

        @pl.when((i >= first) & (i < first + nb))
        def _():
            r = route_ref[0]
            pos1, pos2 = r[4:5], r[5:6]
            rows = lax.broadcasted_iota(jnp.int32, (MOE_ROWS, MOE_BLOCK), 0).astype(F32)
            onehot = jnp.where((rows == pos1) | (rows == pos2), 1.0, 0.0).astype(BF16)
            comp_ref[slot] = _dot(onehot, h_ref[...]).astype(BF16)
        first += nb
    block_copies(base, slot, lambda c: c.start())

    @pl.when(i > 0)
    def _():
        block_copies(base - N_EXPERTS, 1 - slot, lambda c: c.wait())

    @pl.when(i == last)
    def _():
        block_copies(base, slot, lambda c: c.wait())
        zero_copies(lambda c: c.wait())


def _dispatch(tables, hs, routes, n_rows, min_tiles):
    group_blocks = tuple(h.shape[0] // MOE_BLOCK for h in hs)
    in_specs, args, first = [], [], 0
    for h, route, nb in zip(hs, routes, group_blocks):
        blk = lambda i, first=first, nb=nb: jnp.clip(i - first, 0, nb - 1)
        in_specs += [pl.BlockSpec((MOE_BLOCK, D_MODEL), lambda i, *_, blk=blk: (blk(i), 0)),
                     pl.BlockSpec((1, 8, MOE_BLOCK), lambda i, *_, blk=blk: (blk(i), 0, 0))]
        args += [h, route]
        first += nb
    grid_spec = pltpu.PrefetchScalarGridSpec(
        num_scalar_prefetch=len(tables),
        grid=(sum(group_blocks),),
        in_specs=in_specs,
        out_specs=pl.BlockSpec(memory_space=pl.ANY),
        scratch_shapes=[pltpu.VMEM((2, MOE_ROWS, D_MODEL), BF16),
                        pltpu.VMEM((EXPERT_TILE, D_MODEL), BF16),
                        pltpu.SemaphoreType.DMA((3,))],
    )
    return pl.pallas_call(
        functools.partial(_dispatch_kernel, group_blocks=group_blocks, min_tiles=min_tiles),
        grid_spec=grid_spec,
        out_shape=jax.ShapeDtypeStruct((n_rows, D_MODEL), BF16),
        compiler_params=_cparams(1),
        name="moe_dispatch",
    )(*tables, *args)


def _expert_kernel(te_ref, nt_ref, catch_ref, bge_ref, bgc_ref, xs_ref, wgu_hbm, wd_hbm, ys_ref,
                   wgu_bf, wd_bf, stage_gu, stage_d, sem):
    i = pl.program_id(0)
    used = i < nt_ref[0]
    rg, rd = D_MODEL // W_CHUNKS, D_FF // W_CHUNKS

    def chunk_copies(expert, c, slot):
        return (pltpu.make_async_copy(wgu_hbm.at[expert, pl.ds(pl.multiple_of(c * rg, rg), rg)],
                                      stage_gu.at[slot], sem.at[0, slot]),
                pltpu.make_async_copy(wd_hbm.at[expert, pl.ds(pl.multiple_of(c * rd, rd), rd)],
                                      stage_d.at[slot], sem.at[1, slot]))

    def cast_chunk(expert, c, slot):
        ws = expert % 2
        wgu_bf[ws, pl.ds(pl.multiple_of(c * rg, rg), rg), :] = stage_gu[slot].astype(BF16)
        wd_bf[ws, pl.ds(pl.multiple_of(c * rd, rd), rd), :] = stage_d[slot].astype(BF16)

    @pl.when(used)
    def _():
        e = te_ref[i]

        @pl.when(i == 0)
        def _():
            for cp in chunk_copies(bge_ref[0], bgc_ref[0], 0):
                cp.start()

        @pl.when(i + 1 < nt_ref[0])
        def _():
            for cp in chunk_copies(bge_ref[i + 1], bgc_ref[i + 1], (i + 1) % 2):
                cp.start()

        def catch_up(c, carry):
            cps = chunk_copies(e, c, 2)
            for cp in cps:
                cp.start()
            for cp in cps:
                cp.wait()
            cast_chunk(e, c, 2)
            return carry

        lax.fori_loop(catch_ref[i], W_CHUNKS, catch_up, 0)

        for cp in chunk_copies(bge_ref[i], bgc_ref[i], i % 2):
            cp.wait()
        cast_chunk(bge_ref[i], bgc_ref[i], i % 2)
        ws = e % 2
        ys_ref[...] = _swiglu(xs_ref[...], wgu_bf.at[ws], wd_bf.at[ws]).astype(ys_ref.dtype)

    @pl.when(jnp.logical_not(used))
    def _():
        ys_ref[...] = jnp.zeros_like(ys_ref)


def _experts(tile_tables, xs, wgu_e, wd_e):
    rows = xs.shape[0]
    tm = EXPERT_TILE
    rg, rd = D_MODEL // W_CHUNKS, D_FF // W_CHUNKS
    grid_spec = pltpu.PrefetchScalarGridSpec(
        num_scalar_prefetch=len(tile_tables),
        grid=(rows // tm,),
        in_specs=[pl.BlockSpec((tm, D_MODEL), lambda i, te, nt, *_: (jnp.minimum(i, nt[0] - 1), 0)),
                  pl.BlockSpec(memory_space=pl.ANY), pl.BlockSpec(memory_space=pl.ANY)],
        out_specs=pl.BlockSpec((tm, D_MODEL), lambda i, *_: (i, 0)),
        scratch_shapes=[pltpu.VMEM((2, D_MODEL, 2 * D_FF), BF16), pltpu.VMEM((2, D_FF, D_MODEL), BF16),
                        pltpu.VMEM((3, rg, 2 * D_FF), F32), pltpu.VMEM((3, rd, D_MODEL), F32),
                        pltpu.SemaphoreType.DMA((2, 3))],
    )
    return pl.pallas_call(
        _expert_kernel,
        grid_spec=grid_spec,
        out_shape=jax.ShapeDtypeStruct((rows, D_MODEL), BF16),
        compiler_params=_cparams(1),
        name="moe_experts",
    )(*tile_tables, xs, wgu_e, wd_e)


def _combine_kernel(seg_ref, dst_ref, n16_ref, x_ref, rt_ref, mod_ref, g_ref, ys_ref, out_ref,
                    buf_ref, sem, *, block0):
    i = pl.program_id(0)
    slot = i % 2
    base = (block0 + i) * N_EXPERTS

    def block_copies(base_, slot_, act):
        for e in range(N_EXPERTS):
            _segment_copies(
                n16_ref[base_ + e], seg_ref[base_ + e], dst_ref[base_ + e],
                lambda s, d, n: act(pltpu.make_async_copy(
                    ys_ref.at[pl.ds(d, n)], buf_ref.at[slot_, pl.ds(s, n)], sem.at[slot_])))

    def fetch(base_, slot_):
        buf_ref[slot_] = jnp.zeros(buf_ref.shape[1:], buf_ref.dtype)
        block_copies(base_, slot_, lambda c: c.start())

    @pl.when(i == 0)
    def _():
        fetch(base, slot)

    @pl.when(i + 1 < pl.num_programs(0))
    def _():
        fetch(base + N_EXPERTS, 1 - slot)

    rt = rt_ref[...]
    pos1, pos2 = rt[:, 4:5], rt[:, 5:6]
    cols = lax.broadcasted_iota(jnp.int32, (MOE_BLOCK, MOE_ROWS), 1).astype(F32)
    gates = (jnp.where(cols == pos1, rt[:, 2:3], 0.0)
             + jnp.where(cols == pos2, rt[:, 3:4], 0.0)).astype(BF16)
    block_copies(base, slot, lambda c: c.wait())
    x = x_ref[...] + mod_ref[0, 5:6, :] * _dot(gates, buf_ref[slot])
    out_ref[...] = _rms(x, g_ref[...], EPS)


def _combine(tables, x2d, route_t, mod, g, ys, seq_len, row0, block0):
    t = x2d.shape[0]
    tb = MOE_BLOCK
    grid_spec = pltpu.PrefetchScalarGridSpec(
        num_scalar_prefetch=len(tables),
        grid=(t // tb,),
        in_specs=[pl.BlockSpec((tb, D_MODEL), lambda i, *_: (i, 0)),
                  pl.BlockSpec((tb, 8), lambda i, *_: (i, 0)),
                  _mod_spec(max(seq_len // tb, 1), row0),
                  pl.BlockSpec((1, D_MODEL), lambda i, *_: (0, 0)),
                  pl.BlockSpec(memory_space=pl.ANY)],
        out_specs=pl.BlockSpec((tb, D_MODEL), lambda i, *_: (i, 0)),
        scratch_shapes=[pltpu.VMEM((2, MOE_ROWS, D_MODEL), BF16), pltpu.SemaphoreType.DMA((2,))],
    )
    return pl.pallas_call(
        functools.partial(_combine_kernel, block0=block0),
        grid_spec=grid_spec,
        out_shape=jax.ShapeDtypeStruct((t, D_MODEL), F32),
        compiler_params=_cparams(1),
        name="moe_combine",
    )(*tables, x2d, route_t, mod, g.reshape(1, D_MODEL), ys)


def _moe_tables(counts, n_rows):
    pad = BF16_SUBLANES
    n16 = (counts + pad - 1) // pad * pad
    seg = jnp.cumsum(n16, axis=1) - n16
    total = jnp.sum(n16, axis=0)
    region = (total + EXPERT_TILE - 1) // EXPERT_TILE * EXPERT_TILE
    region_end = jnp.cumsum(region)
    region_start = region_end - region
    dst = region_start[None, :] + jnp.cumsum(n16, axis=0) - n16
    tiles_end = region_end // EXPERT_TILE
    tile_ids = jnp.arange(n_rows // EXPERT_TILE, dtype=jnp.int32)
    tile_expert = jnp.minimum(jnp.sum(tile_ids[:, None] >= tiles_end[None, :], axis=1), N_EXPERTS - 1)
    n_tiles_e = region // EXPERT_TILE
    local = tile_ids - (tiles_end - n_tiles_e)[tile_expert]
    prev_tiles = jnp.concatenate([jnp.zeros((1,), n_tiles_e.dtype), n_tiles_e[:-1]])
    catch_from = jnp.where(local == 0, jnp.minimum(prev_tiles[tile_expert], W_CHUNKS), W_CHUNKS)
    ahead_expert = jnp.where(tile_expert < N_EXPERTS - 1, tile_expert + 1, N_EXPERTS - 2)
    ahead_chunk = jnp.clip(local, 0, W_CHUNKS - 1)
    flat = lambda a: a.reshape(-1).astype(jnp.int32)
    block_tables = (flat(seg), flat(dst), flat(n16))
    pad_tables = (flat(region_start + total), flat(region - total))
    tile_tables = (flat(tile_expert), flat(tiles_end[-1:]), flat(catch_from), flat(ahead_expert),
                   flat(ahead_chunk))
    return block_tables, pad_tables, tile_tables


def kernel(x_prompt, x_sample, c, cache_k_0, cache_v_0, c_ctx, ada_w_0, ada_b_0, norm1_g_0, norm2_g_0, w_qkv_0, lambda_q1_0, lambda_k1_0, lambda_q2_0, lambda_k2_0, subln_g_0, w_o_0, w_gu_0, w_down_0, ada_w_1, ada_b_1, norm1_g_1, norm2_g_1, w_fourier_1, w_router_1, w_gu_e_1, w_down_e_1, final_norm_g):
    bp, lp, _ = x_prompt.shape
    bs, ls, _ = x_sample.shape
    assert 1 + bs <= ADA_ROWS and (bp * lp) % MOE_BLOCK == 0 and ls % MOE_BLOCK == 0
    assert MOE_BLOCK % lp == 0 or lp % MOE_BLOCK == 0

    cond = jnp.zeros((ADA_ROWS, D_MODEL), F32).at[0].set(c_ctx).at[1:1 + bs].set(c)
    mod0 = _adaln(cond, ada_w_0, ada_b_0)
    mod1 = _adaln(cond, ada_w_1, ada_b_1)
    lam_params = jnp.stack([lambda_q1_0, lambda_k1_0, lambda_q2_0, lambda_k2_0])

    w_qkv = w_qkv_0.astype(BF16)
    w_o = w_o_0.astype(BF16)
    w_gu = w_gu_0.astype(BF16)
    w_down = w_down_0.astype(BF16)
    w_f = w_fourier_1.astype(BF16)
    w_router_t = w_router_1.T

    groups = [dict(x=x_prompt.reshape(bp * lp, D_MODEL), batch=bp, seq=lp, row0=0, rope=False),
              dict(x=x_sample.reshape(bs * ls, D_MODEL), batch=bs, seq=ls, row0=1, rope=True)]

    k_ctx = v_ctx = None
    for gr in groups:
        x, batch, seq, row0 = gr["x"], gr["batch"], gr["seq"], gr["row0"]
        q, k, v = _qkv(x, mod0, norm1_g_0, w_qkv, seq, row0, gr["rope"], BF16 if gr["rope"] else F32)
        if gr["rope"]:
            o = _attention(lam_params, subln_g_0, q, k, v, batch, seq, cache_k_0, cache_v_0,
                           tq=SAMPLE_Q_TILE, heads=SAMPLE_HEADS, sub=SAMPLE_Q_SUB, lag=ATTN_LAG)
        else:
            k_ctx, v_ctx = k, v
            o = _attention(lam_params, subln_g_0, q, k, v, batch, seq, tq=seq, heads=N_HEADS, sub=seq,
                           lag=ATTN_LAG)
        x, fa, fb = _post_attn(x, o, mod0, mod1, norm2_g_0, norm1_g_1, w_o, w_gu, w_down, seq, row0)
        gr["x"], gr["h"], gr["route"], gr["cnt"] = _fourier_router(
            x, fa, fb, mod1, w_f, norm2_g_1, w_router_t, batch, seq, row0)

    n_blocks = [gr["x"].shape[0] // MOE_BLOCK for gr in groups]
    n_pairs = 2 * sum(gr["x"].shape[0] for gr in groups)
    max_rows = n_pairs + sum(n_blocks) * N_EXPERTS * (BF16_SUBLANES - 1) + N_EXPERTS * EXPERT_TILE
    max_rows = (max_rows + EXPERT_TILE - 1) // EXPERT_TILE * EXPERT_TILE
    counts = jnp.concatenate([gr["cnt"][:, :, 0] for gr in groups], axis=0).astype(jnp.int32)
    block_tables, pad_tables, tile_tables = _moe_tables(counts, max_rows)

    xs = _dispatch(block_tables + pad_tables + tile_tables[1:2], [gr["h"] for gr in groups],
                   [gr["route"] for gr in groups], max_rows, n_pairs // EXPERT_TILE)
    ys = _experts(tile_tables, xs, w_gu_e_1, w_down_e_1)
    outs = []
    block0 = 0
    for gr, nb in zip(groups, n_blocks):
        route_t = gr["route"].transpose(0, 2, 1).reshape(-1, 8)
        outs.append(_combine(block_tables, gr["x"], route_t, mod1, final_norm_g, ys,
                             gr["seq"], gr["row0"], block0))
        block0 += nb

    y_prompt = outs[0].reshape(bp, lp, D_MODEL)
    y_sample = outs[1].reshape(bs, ls, D_MODEL)
    return (y_prompt, y_sample,
            k_ctx.reshape(bp, lp, N_HEADS, 2 * HEAD_DIM), v_ctx.reshape(bp, lp, N_HEADS, V_DIM))
```

```python
import functools
import math

import jax
import jax.numpy as jnp
import numpy as np
from jax import lax
from jax.experimental import pallas as pl
from jax.experimental.pallas import tpu as pltpu

F32 = jnp.float32
BF16 = jnp.bfloat16

D_MODEL = 1024
N_HEADS = 8
HEAD_DIM = 64
V_DIM = 2 * HEAD_DIM
GRID_W = 64
AXIS_DIM = HEAD_DIM // 2
ROPE_THETA = 10000.0
N_FOURIER_GROUPS = 4
FOURIER_GROUP = D_MODEL // N_FOURIER_GROUPS
D_FF = 2816
N_EXPERTS = 8
N_MOD = 6
EPS = 1e-6
SUBLN_EPS = 1e-5
LAMBDA_INIT_0 = 0.8 - 0.6 * math.exp(-0.3 * 0)
Q_SCALE = HEAD_DIM ** -0.5 * math.log2(math.e)

V7X_VMEM_BYTES = 64 * 1024 * 1024
VMEM_LIMIT = V7X_VMEM_BYTES - 8 * 1024 * 1024
V7X_MXU_DIM = 256
BF16_SUBLANES = 16

ADA_ROWS = 16
ADA_TN = 1536
QKV_TILE = 512
FFN_TILE = 512
FF_SPLITS = (0, 6 * V7X_MXU_DIM, D_FF)
SAMPLE_Q_TILE = 2048
SAMPLE_HEADS = 1
SAMPLE_Q_SUB = 128
ATTN_LAG = 1
MOE_BLOCK = 512
MOE_ROWS = 2 * MOE_BLOCK + 128
EXPERT_TILE = 256
W_CHUNKS = 16
SEG_BITS = (512, 256, 128, 64, 32, 16)

_NT = (((1,), (1,)), ((), ()))


def _dot(a, b):
    return jnp.dot(a, b, preferred_element_type=F32)


def _split_bf16(x):
    hi = x.astype(BF16)
    lo = (x - hi.astype(F32)).astype(BF16)
    return hi, lo


def _rms(x, g, eps):
    return x * lax.rsqrt(jnp.mean(x * x, axis=-1, keepdims=True) + eps) * g


def _cparams(n_grid, vmem=VMEM_LIMIT):
    return pltpu.CompilerParams(dimension_semantics=("arbitrary",) * n_grid, vmem_limit_bytes=vmem)


def _const_spec(shape):
    nd = len(shape)
    return pl.BlockSpec(shape, lambda *_: (0,) * nd, pipeline_mode=pl.Buffered(1))


def _adaln_kernel(c_ref, w_ref, b_ref, o_ref):
    c = c_ref[...]
    a_hi, a_lo = _split_bf16(c * jax.nn.sigmoid(c))
    w_hi, w_lo = _split_bf16(w_ref[...])
    o_ref[...] = _dot(a_hi, w_hi) + _dot(a_hi, w_lo) + _dot(a_lo, w_hi) + b_ref[...]


def _adaln(cond, w, b):
    n = N_MOD * D_MODEL
    out = pl.pallas_call(
        _adaln_kernel,
        grid=(n // ADA_TN,),
        in_specs=[pl.BlockSpec((ADA_ROWS, D_MODEL), lambda j: (0, 0)),
                  pl.BlockSpec((D_MODEL, ADA_TN), lambda j: (0, j)),
                  pl.BlockSpec((1, ADA_TN), lambda j: (0, j))],
        out_specs=pl.BlockSpec((ADA_ROWS, ADA_TN), lambda j: (0, j)),
        out_shape=jax.ShapeDtypeStruct((ADA_ROWS, n), F32),
        compiler_params=_cparams(1),
        name="adaln",
    )(cond, w, b.reshape(1, n))
    return out.reshape(ADA_ROWS, N_MOD, D_MODEL)


def _mod_spec(seq_tiles, row0):
    if row0 == 0:
        return pl.BlockSpec((1, N_MOD, D_MODEL), lambda i, *_: (0, 0, 0))
    return pl.BlockSpec((1, N_MOD, D_MODEL), lambda i, *_: (row0 + i // seq_tiles, 0, 0))


def _qkv_kernel(*refs, rope):
    if rope:
        x_ref, mod_ref, g_ref, w_ref, cos_ref, sa_ref, sb_ref, q_ref, k_ref, v_ref = refs
    else:
        x_ref, mod_ref, g_ref, w_ref, q_ref, k_ref, v_ref = refs
    h = _rms(x_ref[...], g_ref[...], EPS)
    h = h * (1.0 + mod_ref[0, 1:2, :]) + mod_ref[0, 0:1, :]
    qkv = _dot(h.astype(BF16), w_ref[...])
    inner = N_HEADS * 2 * HEAD_DIM
    for which, ref in ((0, q_ref), (1, k_ref), (2, v_ref)):
        part = qkv[:, which * inner:(which + 1) * inner]
        if rope and which < 2:
            cos, sa, sb = cos_ref[...], sa_ref[...], sb_ref[...]
            for hd in range(N_HEADS):
                blk = part[:, hd * V_DIM:(hd + 1) * V_DIM]
                blk = (blk * cos + pltpu.roll(blk, V_DIM - AXIS_DIM // 2, 1) * sa
                       + pltpu.roll(blk, AXIS_DIM // 2, 1) * sb)
                if which == 0:
                    blk = blk * Q_SCALE
                ref[:, hd * V_DIM:(hd + 1) * V_DIM] = blk.astype(ref.dtype)
        else:
            if which == 0:
                part = part * Q_SCALE
            ref[...] = part.astype(ref.dtype)


def _rope_tables(length):
    rows = length // GRID_W
    row = jnp.repeat(jnp.arange(rows), GRID_W).astype(F32)
    col = jnp.tile(jnp.arange(GRID_W), rows).astype(F32)
    inv = 1.0 / (ROPE_THETA ** (jnp.arange(0, AXIS_DIM, 2, dtype=F32) / AXIS_DIM))
    ar = row[:, None] * inv[None, :]
    ac = col[:, None] * inv[None, :]
    ang = jnp.concatenate([ar, ar, ac, ac], axis=-1)
    cos, sin = jnp.cos(ang), jnp.sin(ang)
    first = (jnp.arange(HEAD_DIM) % AXIS_DIM) < (AXIS_DIM // 2)
    sin_a = jnp.where(first[None, :], -sin, 0.0)
    sin_b = jnp.where(first[None, :], 0.0, sin)
    wide = lambda t: jnp.concatenate([t, t], axis=-1)
    return wide(cos), wide(sin_a), wide(sin_b)


def _qkv(x2d, mod, g, w_bf16, seq_len, row0, rope, kv_dtype):
    t = x2d.shape[0]
    tm = QKV_TILE
    seq_tiles = seq_len // tm
    row_spec = pl.BlockSpec((tm, D_MODEL), lambda i: (i, 0))
    in_specs = [row_spec, _mod_spec(seq_tiles, row0), _const_spec((1, D_MODEL)),
                _const_spec((D_MODEL, 3 * D_MODEL))]
    args = [x2d, mod, g.reshape(1, D_MODEL), w_bf16]
    if rope:
        tab_spec = pl.BlockSpec((tm, V_DIM), lambda i: (i % seq_tiles, 0))
        in_specs += [tab_spec] * 3
        args += list(_rope_tables(seq_len))
    return pl.pallas_call(
        functools.partial(_qkv_kernel, rope=rope),
        grid=(t // tm,),
        in_specs=in_specs,
        out_specs=[row_spec] * 3,
        out_shape=[jax.ShapeDtypeStruct((t, D_MODEL), BF16),
                   jax.ShapeDtypeStruct((t, D_MODEL), kv_dtype),
                   jax.ShapeDtypeStruct((t, D_MODEL), kv_dtype)],
        compiler_params=_cparams(1),
        name="qkv",
    )(*args)


def _attn_kernel(*refs, has_cache, heads, sub, lag):
    if has_cache:
        lam_ref, sg_ref, q_ref, k_ref, v_ref, ck_ref, cv_ref, o_ref = refs
    else:
        lam_ref, sg_ref, q_ref, k_ref, v_ref, o_ref = refs
    lp = lam_ref[...]
    lam = (jnp.exp(jnp.sum(lp[0:1] * lp[1:2], axis=-1, keepdims=True))
           - jnp.exp(jnp.sum(lp[2:3] * lp[3:4], axis=-1, keepdims=True)) + LAMBDA_INIT_0)
    lane = lax.broadcasted_iota(jnp.int32, (1, V_DIM), 1)
    sg = sg_ref[...] * (1.0 - LAMBDA_INIT_0)
    def head_keys(hd):
        cols = slice(hd * V_DIM, (hd + 1) * V_DIM)
        keys = [(k_ref[0, :, cols].astype(BF16), v_ref[0, :, cols].astype(BF16))]
        if has_cache:
            keys.append((ck_ref[0, :, cols].astype(BF16), cv_ref[0, :, cols].astype(BF16)))
        return keys

    chains = [(hd, r0, comp) for hd in range(heads) for r0 in range(0, q_ref.shape[0], sub)
              for comp in range(2)]
    keys = {hd: head_keys(hd) for hd in range(heads)}
    scores, probs, outs = {}, {}, {}

    def stage_scores(c):
        hd, r0, comp = chains[c]
        q = q_ref[r0:r0 + sub, hd * V_DIM:(hd + 1) * V_DIM]
        sel = (lane < HEAD_DIM) if comp == 0 else (lane >= HEAD_DIM)
        qc = jnp.where(sel, q, jnp.zeros_like(q))
        scores[c] = [lax.dot_general(qc, k, _NT, preferred_element_type=F32) for k, _ in keys[hd]]

    def stage_softmax(c):
        s = scores.pop(c)
        m = functools.reduce(jnp.maximum, [jnp.max(x, axis=-1, keepdims=True) for x in s])
        p = [jnp.exp2(x - m) for x in s]
        l = functools.reduce(jnp.add, [jnp.sum(x, axis=-1, keepdims=True) for x in p])
        probs[c] = ([x.astype(BF16) for x in p], l)

    def stage_values(c):
        hd, r0, comp = chains[c]
        p, l = probs.pop(c)
        o = functools.reduce(jnp.add, [_dot(x, v) for x, (_, v) in zip(p, keys[hd])])
        outs[c] = o * (1.0 / l)
        if comp == 1:
            o = outs.pop(c - 1) - lam * outs.pop(c)
            o_ref[r0:r0 + sub, hd * V_DIM:(hd + 1) * V_DIM] = _rms(o, sg, SUBLN_EPS).astype(o_ref.dtype)

    for t in range(len(chains) + 2 * lag):
        if t < len(chains):
            stage_scores(t)
        if 0 <= t - lag < len(chains):
            stage_softmax(t - lag)
        if 0 <= t - 2 * lag < len(chains):
            stage_values(t - 2 * lag)


def _attention(lam_params, subln_g, q, k, v, batch, seq_len, cache_k=None, cache_v=None, *,
               tq, heads, sub, lag):
    nq = seq_len // tq
    has_cache = cache_k is not None
    width = heads * V_DIM
    k3 = k.reshape(batch, seq_len, D_MODEL)
    v3 = v.reshape(batch, seq_len, D_MODEL)
    q_spec = pl.BlockSpec((tq, width), lambda b, h, i: (b * nq + i, h))
    kv_spec = pl.BlockSpec((1, seq_len, width), lambda b, h, i: (b, 0, h))
    in_specs = [_const_spec((4, HEAD_DIM)), _const_spec((1, V_DIM)), q_spec, kv_spec, kv_spec]
    args = [lam_params, subln_g.reshape(1, V_DIM), q, k3, v3]
    if has_cache:
        past = cache_k.shape[1]
        c_spec = pl.BlockSpec((1, past, width), lambda b, h, i: (b, 0, h))
        in_specs += [c_spec, c_spec]
        args += [cache_k.reshape(batch, past, D_MODEL), cache_v.reshape(batch, past, D_MODEL)]
    return pl.pallas_call(
        functools.partial(_attn_kernel, has_cache=has_cache, heads=heads, sub=sub, lag=lag),
        grid=(batch, N_HEADS // heads, nq),
        in_specs=in_specs,
        out_specs=q_spec,
        out_shape=jax.ShapeDtypeStruct((batch * seq_len, D_MODEL), BF16),
        compiler_params=_cparams(3),
        name="diff_attn",
    )(*args)


def _swiglu(h, wgu, wd):
    out = None
    for c0, c1 in zip(FF_SPLITS[:-1], FF_SPLITS[1:]):
        g = _dot(h, wgu[:, c0:c1])
        u = _dot(h, wgu[:, D_FF + c0:D_FF + c1])
        down = _dot((g * jax.nn.sigmoid(g) * u).astype(BF16), wd[c0:c1, :])
        out = down if out is None else out + down
    return out


def _dft_tables(n):
    j = np.arange(n, dtype=np.int64)
    ang = (2.0 * np.pi / n) * ((j[:, None] * j[None, :]) % n).astype(np.float64)
    s = n ** -0.5
    return np.cos(ang) * s, np.sin(ang) * s


def _post_attn_kernel(x_ref, o_ref, mod0_ref, mod1_ref, g2_ref, g1n_ref, wo_ref, wgu_ref, wd_ref,
                      cs_ref, out_ref, a_ref, b_ref, *, pair_rows):
    x1 = x_ref[...] + mod0_ref[0, 2:3, :] * _dot(o_ref[...], wo_ref[...])
    h = _rms(x1, g2_ref[...], EPS) * (1.0 + mod0_ref[0, 4:5, :]) + mod0_ref[0, 3:4, :]
    x2 = x1 + mod0_ref[0, 5:6, :] * _swiglu(h.astype(BF16), wgu_ref, wd_ref)
    out_ref[...] = x2
    h1 = _rms(x2, g1n_ref[...], EPS) * (1.0 + mod1_ref[0, 1:2, :]) + mod1_ref[0, 0:1, :]
    hb = h1.astype(BF16)
    fg = FOURIER_GROUP
    n = hb.shape[0]
    if pair_rows:
        row = lax.broadcasted_iota(jnp.int32, (n, n), 0)
        tok = lax.broadcasted_iota(jnp.int32, (n, n), 1)
        src = jnp.where(row < n // 2, 2 * row, 2 * row - (n - 1))
        hb = _dot(jnp.where(tok == src, 1.0, 0.0).astype(BF16), hb).astype(BF16)
    for grp in range(N_FOURIER_GROUPS):
        ab = _dot(hb[:, grp * fg:(grp + 1) * fg], cs_ref[...])
        cols = slice(grp * fg, (grp + 1) * fg)
        if pair_rows:
            odd_cols = slice(D_MODEL + grp * fg, D_MODEL + (grp + 1) * fg)
            a_ref[:, cols] = ab[:n // 2, :fg].astype(BF16)
            a_ref[:, odd_cols] = ab[n // 2:, :fg].astype(BF16)
            b_ref[:, cols] = ab[:n // 2, fg:].astype(BF16)
            b_ref[:, odd_cols] = ab[n // 2:, fg:].astype(BF16)
        else:
            a_ref[:, cols] = ab[:, :fg].astype(BF16)
            b_ref[:, cols] = ab[:, fg:].astype(BF16)


def _post_attn(x2d, o, mod0, mod1, g2, g1n, wo, wgu, wd, seq_len, row0, pair_rows):
    t = x2d.shape[0]
    tm = FFN_TILE if row0 == 0 else min(FFN_TILE, seq_len)
    cd, sd = _dft_tables(FOURIER_GROUP)
    cs = jnp.asarray(np.concatenate([cd, sd], axis=1), dtype=F32).astype(BF16)
    row_spec = pl.BlockSpec((tm, D_MODEL), lambda i: (i, 0))
    mod_spec = _mod_spec(seq_len // tm, row0)
    if pair_rows:
        ab_spec = pl.BlockSpec((tm // 2, 2 * D_MODEL), lambda i: (i, 0))
        ab_shape = jax.ShapeDtypeStruct((t // 2, 2 * D_MODEL), BF16)
    else:
        ab_spec, ab_shape = row_spec, jax.ShapeDtypeStruct((t, D_MODEL), BF16)
    return pl.pallas_call(
        functools.partial(_post_attn_kernel, pair_rows=pair_rows),
        grid=(t // tm,),
        in_specs=[row_spec, row_spec, mod_spec, mod_spec, _const_spec((1, D_MODEL)),
                  _const_spec((1, D_MODEL)), _const_spec((D_MODEL, D_MODEL)),
                  _const_spec((D_MODEL, 2 * D_FF)), _const_spec((D_FF, D_MODEL)),
                  _const_spec((FOURIER_GROUP, 2 * FOURIER_GROUP))],
        out_specs=[row_spec, ab_spec, ab_spec],
        out_shape=[jax.ShapeDtypeStruct((t, D_MODEL), F32), ab_shape, ab_shape],
        compiler_params=_cparams(1),
        name="post_attn_swiglu",
    )(x2d, o, mod0, mod1, g2.reshape(1, D_MODEL), g1n.reshape(1, D_MODEL), wo, wgu, wd, cs)


def _route(x, mod_ref, g_ref, wr_ref, h_ref, route_ref, cnt_ref):
    tb = MOE_BLOCK
    h = _rms(x, g_ref[...], EPS) * (1.0 + mod_ref[0, 4:5, :]) + mod_ref[0, 3:4, :]
    h_hi, h_lo = _split_bf16(h)
    h_ref[...] = h_hi
    w_hi, w_lo = _split_bf16(wr_ref[...])
    dg = lambda a, b: lax.dot_general(a, b, _NT, preferred_element_type=F32)
    logits = dg(w_hi, h_hi) + dg(w_lo, h_hi) + dg(w_hi, h_lo)
    e = jnp.exp(logits - jnp.max(logits, axis=0, keepdims=True))
    probs = e / jnp.sum(e, axis=0, keepdims=True)
    eidx = lax.broadcasted_iota(jnp.int32, (N_EXPERTS, tb), 0).astype(F32)
    big = float(N_EXPERTS)
    p1 = jnp.max(probs, axis=0, keepdims=True)
    i1 = jnp.min(jnp.where(probs == p1, eidx, big), axis=0, keepdims=True)
    oh1 = eidx == i1
    rest = jnp.where(oh1, -1.0, probs)
    p2 = jnp.max(rest, axis=0, keepdims=True)
    i2 = jnp.min(jnp.where(rest == p2, eidx, big), axis=0, keepdims=True)
    oh2 = eidx == i2
    den = p1 + p2
    oh = jnp.where(oh1 | oh2, 1.0, 0.0)
    before = (lax.broadcasted_iota(jnp.int32, (tb, tb), 0)
              < lax.broadcasted_iota(jnp.int32, (tb, tb), 1))
    rank = _dot(oh.astype(BF16), jnp.where(before, 1.0, 0.0).astype(BF16))
    cnt = jnp.sum(oh, axis=1, keepdims=True)
    cnt_ref[0] = jnp.broadcast_to(cnt, (N_EXPERTS, 128))
    n16 = jnp.floor((cnt + (BF16_SUBLANES - 1.0)) * (1.0 / BF16_SUBLANES)) * BF16_SUBLANES
    ecol = lax.broadcasted_iota(jnp.int32, (N_EXPERTS, 1), 0)
    seg = jnp.zeros_like(n16)
    for ex in range(N_EXPERTS - 1):
        seg = seg + jnp.where(ecol > ex, n16[ex:ex + 1, :], 0.0)
    pos = rank + seg
    r1 = jnp.sum(jnp.where(oh1, pos, 0.0), axis=0, keepdims=True)
    r2 = jnp.sum(jnp.where(oh2, pos, 0.0), axis=0, keepdims=True)
    zero = jnp.zeros_like(r1)
    route_ref[...] = jnp.concatenate([i1, i2, p1 / den, p2 / den, r1, r2, zero, zero], axis=0)


def _fourier_router_kernel(x_ref, mod_ref, cl_ref, sl_ref, a_ref, b_ref, wf_ref, g_ref, wr_ref,
                           out_ref, h_ref, route_ref, cnt_ref):
    ys = [_dot(cl_ref[...], a_ref[j]) + _dot(sl_ref[...], b_ref[j]) for j in range(a_ref.shape[0])]
    y = ys[0] if len(ys) == 1 else jnp.concatenate(ys, axis=0)
    x = x_ref[...] + mod_ref[0, 2:3, :] * _dot(y.astype(BF16), wf_ref[...])
    out_ref[...] = x
    _route(x, mod_ref, g_ref, wr_ref, h_ref, route_ref.at[0], cnt_ref)


def _fourier_split_router_kernel(x_ref, mod_ref, ce_ref, se_ref, co_ref, so_ref, a_ref, b_ref, wf_ref,
                                 g_ref, wr_ref, out_ref, h_ref, route_ref, cnt_ref):
    half_len = a_ref.shape[2] // 2
    even = _dot(ce_ref[...], a_ref[0, :, :half_len]) + _dot(se_ref[...], b_ref[0, :, :half_len])
    odd = _dot(co_ref[...], a_ref[0, :, half_len:]) + _dot(so_ref[...], b_ref[0, :, half_len:])
    for half, y in ((0, even + odd), (1, even - odd)):
        x = x_ref[0, half] + mod_ref[0, 2:3, :] * _dot(y.astype(BF16), wf_ref[...])
        out_ref[0, half] = x
        _route(x, mod_ref, g_ref, wr_ref, h_ref.at[0, half], route_ref.at[0, half, 0],
               cnt_ref.at[0, half])


def _fourier_split_router(x2d, a, b, mod, wf, g, wr_t, batch, seq_len, row0):
    t = x2d.shape[0]
    rows = MOE_BLOCK
    half_len = seq_len // 2
    nr = half_len // rows
    j = np.arange(half_len, dtype=np.int64)[:, None]
    m = np.arange(half_len, dtype=np.int64)[None, :]
    scale = seq_len ** -0.5

    def table(fn, k, sign):
        ang = (2.0 * np.pi / seq_len) * ((j * k) % seq_len).astype(np.float64)
        return jnp.asarray(sign * scale * fn(ang), dtype=F32).astype(BF16)

    tables = [table(np.cos, 2 * m, 1.0), table(np.sin, 2 * m, -1.0),
              table(np.cos, 2 * m + 1, 1.0), table(np.sin, 2 * m + 1, -1.0)]
    row4 = pl.BlockSpec((1, 2, rows, D_MODEL), lambda i: (i // nr, 0, i % nr, 0))
    tab_spec = pl.BlockSpec((rows, half_len), lambda i: (i % nr, 0))
    ab_spec = pl.BlockSpec((1, half_len, 2 * D_MODEL), lambda i: (i // nr, 0, 0))
    shape4 = (batch, 2, half_len, D_MODEL)
    out, h, route, cnt = pl.pallas_call(
        _fourier_split_router_kernel,
        grid=(batch * nr,),
        in_specs=[row4, _mod_spec(nr, row0)] + [tab_spec] * 4 + [ab_spec, ab_spec,
                  _const_spec((D_MODEL, D_MODEL)), _const_spec((1, D_MODEL)),
                  _const_spec((N_EXPERTS, D_MODEL))],
        out_specs=[row4, row4,
                   pl.BlockSpec((1, 2, 1, 8, rows), lambda i: (i // nr, 0, i % nr, 0, 0)),
                   pl.BlockSpec((1, 2, 1, N_EXPERTS, 128), lambda i: (i // nr, 0, i % nr, 0, 0))],
        out_shape=[jax.ShapeDtypeStruct(shape4, F32), jax.ShapeDtypeStruct(shape4, BF16),
                   jax.ShapeDtypeStruct((batch, 2, nr, 8, rows), F32),
                   jax.ShapeDtypeStruct((batch, 2, nr, N_EXPERTS, 128), F32)],
        compiler_params=_cparams(1),
        name="fourier_split_router",
    )(x2d.reshape(shape4), mod, *tables, a.reshape(batch, half_len, 2 * D_MODEL),
      b.reshape(batch, half_len, 2 * D_MODEL), wf, g.reshape(1, D_MODEL), wr_t)
    nb = t // rows
    return (out.reshape(t, D_MODEL), h.reshape(t, D_MODEL), route.reshape(nb, 8, rows),
            cnt.reshape(nb, N_EXPERTS, 128))


def _fourier_router(x2d, a, b, mod, wf, g, wr_t, batch, seq_len, row0):
    t = x2d.shape[0]
    rows = MOE_BLOCK
    if a.shape[1] == 2 * D_MODEL:
        return _fourier_split_router(x2d, a, b, mod, wf, g, wr_t, batch, seq_len, row0)
    part = min(seq_len, rows)
    nbat = rows // part
    nr = seq_len // part
    cl, sl = _dft_tables(seq_len)
    cl = jnp.asarray(cl, dtype=F32).astype(BF16)
    sl = jnp.asarray(-sl, dtype=F32).astype(BF16)
    row_spec = pl.BlockSpec((rows, D_MODEL), lambda i: (i, 0))
    tab_spec = pl.BlockSpec((part, seq_len), lambda i: (i % nr, 0))
    ab_spec = pl.BlockSpec((nbat, seq_len, D_MODEL), lambda i: (i // nr, 0, 0))
    nb = t // rows
    return pl.pallas_call(
        _fourier_router_kernel,
        grid=(nb,),
        in_specs=[row_spec, _mod_spec(nr, row0), tab_spec, tab_spec, ab_spec, ab_spec,
                  _const_spec((D_MODEL, D_MODEL)), _const_spec((1, D_MODEL)),
                  _const_spec((N_EXPERTS, D_MODEL))],
        out_specs=[row_spec, row_spec, pl.BlockSpec((1, 8, rows), lambda i: (i, 0, 0)),
                   pl.BlockSpec((1, N_EXPERTS, 128), lambda i: (i, 0, 0))],
        out_shape=[jax.ShapeDtypeStruct((t, D_MODEL), F32),
                   jax.ShapeDtypeStruct((t, D_MODEL), BF16),
                   jax.ShapeDtypeStruct((nb, 8, rows), F32),
                   jax.ShapeDtypeStruct((nb, N_EXPERTS, 128), F32)],
        compiler_params=_cparams(1),
        name="fourier_router",
    )(x2d, mod, cl, sl, a.reshape(batch, seq_len, D_MODEL), b.reshape(batch, seq_len, D_MODEL),
      wf, g.reshape(1, D_MODEL), wr_t)


def _segment_copies(n16, src_row, dst_row, make_copy):
    for bit in SEG_BITS:
        done = n16 & ~(2 * bit - 1)

        @pl.when((n16 & bit) != 0)
        def _():
            make_copy(pl.multiple_of(src_row + done, BF16_SUBLANES),
                      pl.multiple_of(dst_row + done, BF16_SUBLANES), bit)


def _dispatch_kernel(seg_ref, dst_ref, n16_ref, pad_dst_ref, pad_n_ref, nt_ref, *refs,
                     group_blocks, min_tiles):
    n_in = 2 * len(group_blocks)
    xs_ref, comp_ref, zero_ref, sem = refs[n_in:]
    i = pl.program_id(0)
    last = pl.num_programs(0) - 1
    slot = i % 2
    base = i * N_EXPERTS

    def block_copies(base_, slot_, act):
        for e in range(N_EXPERTS):
            _segment_copies(
                n16_ref[base_ + e], seg_ref[base_ + e], dst_ref[base_ + e],
                lambda s, d, n: act(pltpu.make_async_copy(
                    comp_ref.at[slot_, pl.ds(s, n)], xs_ref.at[pl.ds(d, n)], sem.at[slot_])))

    def zero_copies(act):
        for e in range(N_EXPERTS):
            _segment_copies(
                pad_n_ref[e], 0, pad_dst_ref[e],
                lambda s, d, n: act(pltpu.make_async_copy(
                    zero_ref.at[pl.ds(s, n)], xs_ref.at[pl.ds(d, n)], sem.at[2])))
        for tile in range(min_tiles, xs_ref.shape[0] // EXPERT_TILE):
            @pl.when(tile >= nt_ref[0])
            def _():
                act(pltpu.make_async_copy(
                    zero_ref, xs_ref.at[pl.ds(tile * EXPERT_TILE, EXPERT_TILE)], sem.at[2]))

    @pl.when(i == 0)
    def _():
        zero_ref[...] = jnp.zeros_like(zero_ref)
        zero_copies(lambda c: c.start())

    first = 0
    for grp, nb in enumerate(group_blocks):
        h_ref, route_ref = refs[2 * grp], refs[2 * grp + 1]

        @pl.when((i >= first) & (i < first + nb))
        def _():
            r = route_ref[0]
            pos1, pos2 = r[4:5], r[5:6]
            rows = lax.broadcasted_iota(jnp.int32, (MOE_ROWS, MOE_BLOCK), 0).astype(F32)
            onehot = jnp.where((rows == pos1) | (rows == pos2), 1.0, 0.0).astype(BF16)
            comp_ref[slot] = _dot(onehot, h_ref[...]).astype(BF16)
        first += nb
    block_copies(base, slot, lambda c: c.start())

    @pl.when(i > 0)
    def _():
        block_copies(base - N_EXPERTS, 1 - slot, lambda c: c.wait())

    @pl.when(i == last)
    def _():
        block_copies(base, slot, lambda c: c.wait())
        zero_copies(lambda c: c.wait())


def _dispatch(tables, hs, routes, n_rows, min_tiles):
    group_blocks = tuple(h.shape[0] // MOE_BLOCK for h in hs)
    in_specs, args, first = [], [], 0
    for h, route, nb in zip(hs, routes, group_blocks):
        blk = lambda i, first=first, nb=nb: jnp.clip(i - first, 0, nb - 1)
        in_specs += [pl.BlockSpec((MOE_BLOCK, D_MODEL), lambda i, *_, blk=blk: (blk(i), 0)),
                     pl.BlockSpec((1, 8, MOE_BLOCK), lambda i, *_, blk=blk: (blk(i), 0, 0))]
        args += [h, route]
        first += nb
    grid_spec = pltpu.PrefetchScalarGridSpec(
        num_scalar_prefetch=len(tables),
        grid=(sum(group_blocks),),
        in_specs=in_specs,
        out_specs=pl.BlockSpec(memory_space=pl.ANY),
        scratch_shapes=[pltpu.VMEM((2, MOE_ROWS, D_MODEL), BF16),
                        pltpu.VMEM((EXPERT_TILE, D_MODEL), BF16),
                        pltpu.SemaphoreType.DMA((3,))],
    )
    return pl.pallas_call(
        functools.partial(_dispatch_kernel, group_blocks=group_blocks, min_tiles=min_tiles),
        grid_spec=grid_spec,
        out_shape=jax.ShapeDtypeStruct((n_rows, D_MODEL), BF16),
        compiler_params=_cparams(1),
        name="moe_dispatch",
    )(*tables, *args)


def _expert_kernel(te_ref, nt_ref, catch_ref, bge_ref, bgc_ref, xs_ref, wgu_hbm, wd_hbm, ys_ref,
                   wgu_bf, wd_bf, stage_gu, stage_d, sem):
    i = pl.program_id(0)
    used = i < nt_ref[0]
    rg, rd = D_MODEL // W_CHUNKS, D_FF // W_CHUNKS

    def chunk_copies(expert, c, slot):
        return (pltpu.make_async_copy(wgu_hbm.at[expert, pl.ds(pl.multiple_of(c * rg, rg), rg)],
                                      stage_gu.at[slot], sem.at[0, slot]),
                pltpu.make_async_copy(wd_hbm.at[expert, pl.ds(pl.multiple_of(c * rd, rd), rd)],
                                      stage_d.at[slot], sem.at[1, slot]))

    def cast_chunk(expert, c, slot):
        ws = expert % 2
        wgu_bf[ws, pl.ds(pl.multiple_of(c * rg, rg), rg), :] = stage_gu[slot].astype(BF16)
        wd_bf[ws, pl.ds(pl.multiple_of(c * rd, rd), rd), :] = stage_d[slot].astype(BF16)

    @pl.when(used)
    def _():
        e = te_ref[i]

        @pl.when(i == 0)
        def _():
            for cp in chunk_copies(bge_ref[0], bgc_ref[0], 0):
                cp.start()

        @pl.when(i + 1 < nt_ref[0])
        def _():
            for cp in chunk_copies(bge_ref[i + 1], bgc_ref[i + 1], (i + 1) % 2):
                cp.start()

        def catch_up(c, carry):
            cps = chunk_copies(e, c, 2)
            for cp in cps:
                cp.start()
            for cp in cps:
                cp.wait()
            cast_chunk(e, c, 2)
            return carry

        lax.fori_loop(catch_ref[i], W_CHUNKS, catch_up, 0)

        for cp in chunk_copies(bge_ref[i], bgc_ref[i], i % 2):
            cp.wait()
        cast_chunk(bge_ref[i], bgc_ref[i], i % 2)
        ws = e % 2
        ys_ref[...] = _swiglu(xs_ref[...], wgu_bf.at[ws], wd_bf.at[ws]).astype(ys_ref.dtype)

    @pl.when(jnp.logical_not(used))
    def _():
        ys_ref[...] = jnp.zeros_like(ys_ref)


def _experts(tile_tables, xs, wgu_e, wd_e):
    rows = xs.shape[0]
    tm = EXPERT_TILE
    rg, rd = D_MODEL // W_CHUNKS, D_FF // W_CHUNKS
    grid_spec = pltpu.PrefetchScalarGridSpec(
        num_scalar_prefetch=len(tile_tables),
        grid=(rows // tm,),
        in_specs=[pl.BlockSpec((tm, D_MODEL), lambda i, te, nt, *_: (jnp.minimum(i, nt[0] - 1), 0)),
                  pl.BlockSpec(memory_space=pl.ANY), pl.BlockSpec(memory_space=pl.ANY)],
        out_specs=pl.BlockSpec((tm, D_MODEL), lambda i, *_: (i, 0)),
        scratch_shapes=[pltpu.VMEM((2, D_MODEL, 2 * D_FF), BF16), pltpu.VMEM((2, D_FF, D_MODEL), BF16),
                        pltpu.VMEM((3, rg, 2 * D_FF), F32), pltpu.VMEM((3, rd, D_MODEL), F32),
                        pltpu.SemaphoreType.DMA((2, 3))],
    )
    return pl.pallas_call(
        _expert_kernel,
        grid_spec=grid_spec,
        out_shape=jax.ShapeDtypeStruct((rows, D_MODEL), BF16),
        compiler_params=_cparams(1),
        name="moe_experts",
    )(*tile_tables, xs, wgu_e, wd_e)


def _combine_kernel(seg_ref, dst_ref, n16_ref, x_ref, rt_ref, mod_ref, g_ref, ys_ref, out_ref,
                    buf_ref, sem, *, block0):
    i = pl.program_id(0)
    slot = i % 2
    base = (block0 + i) * N_EXPERTS

    def block_copies(base_, slot_, act):
        for e in range(N_EXPERTS):
            _segment_copies(
                n16_ref[base_ + e], seg_ref[base_ + e], dst_ref[base_ + e],
                lambda s, d, n: act(pltpu.make_async_copy(
                    ys_ref.at[pl.ds(d, n)], buf_ref.at[slot_, pl.ds(s, n)], sem.at[slot_])))

    def fetch(base_, slot_):
        buf_ref[slot_] = jnp.zeros(buf_ref.shape[1:], buf_ref.dtype)
        block_copies(base_, slot_, lambda c: c.start())

    @pl.when(i == 0)
    def _():
        fetch(base, slot)

    @pl.when(i + 1 < pl.num_programs(0))
    def _():
        fetch(base + N_EXPERTS, 1 - slot)

    rt = rt_ref[...]
    pos1, pos2 = rt[:, 4:5], rt[:, 5:6]
    cols = lax.broadcasted_iota(jnp.int32, (MOE_BLOCK, MOE_ROWS), 1).astype(F32)
    gates = (jnp.where(cols == pos1, rt[:, 2:3], 0.0)
             + jnp.where(cols == pos2, rt[:, 3:4], 0.0)).astype(BF16)
    block_copies(base, slot, lambda c: c.wait())
    x = x_ref[...] + mod_ref[0, 5:6, :] * _dot(gates, buf_ref[slot])
    out_ref[...] = _rms(x, g_ref[...], EPS)


def _combine(tables, x2d, route_t, mod, g, ys, seq_len, row0, block0):
    t = x2d.shape[0]
    tb = MOE_BLOCK
    grid_spec = pltpu.PrefetchScalarGridSpec(
        num_scalar_prefetch=len(tables),
        grid=(t // tb,),
        in_specs=[pl.BlockSpec((tb, D_MODEL), lambda i, *_: (i, 0)),
                  pl.BlockSpec((tb, 8), lambda i, *_: (i, 0)),
                  _mod_spec(max(seq_len // tb, 1), row0),
                  pl.BlockSpec((1, D_MODEL), lambda i, *_: (0, 0)),
                  pl.BlockSpec(memory_space=pl.ANY)],
        out_specs=pl.BlockSpec((tb, D_MODEL), lambda i, *_: (i, 0)),
        scratch_shapes=[pltpu.VMEM((2, MOE_ROWS, D_MODEL), BF16), pltpu.SemaphoreType.DMA((2,))],
    )
    return pl.pallas_call(
        functools.partial(_combine_kernel, block0=block0),
        grid_spec=grid_spec,
        out_shape=jax.ShapeDtypeStruct((t, D_MODEL), F32),
        compiler_params=_cparams(1),
        name="moe_combine",
    )(*tables, x2d, route_t, mod, g.reshape(1, D_MODEL), ys)


def _moe_tables(counts, n_rows):
    pad = BF16_SUBLANES
    n16 = (counts + pad - 1) // pad * pad
    seg = jnp.cumsum(n16, axis=1) - n16
    total = jnp.sum(n16, axis=0)
    region = (total + EXPERT_TILE - 1) // EXPERT_TILE * EXPERT_TILE
    region_end = jnp.cumsum(region)
    region_start = region_end - region
    dst = region_start[None, :] + jnp.cumsum(n16, axis=0) - n16
    tiles_end = region_end // EXPERT_TILE
    tile_ids = jnp.arange(n_rows // EXPERT_TILE, dtype=jnp.int32)
    tile_expert = jnp.minimum(jnp.sum(tile_ids[:, None] >= tiles_end[None, :], axis=1), N_EXPERTS - 1)
    n_tiles_e = region // EXPERT_TILE
    local = tile_ids - (tiles_end - n_tiles_e)[tile_expert]
    prev_tiles = jnp.concatenate([jnp.zeros((1,), n_tiles_e.dtype), n_tiles_e[:-1]])
    catch_from = jnp.where(local == 0, jnp.minimum(prev_tiles[tile_expert], W_CHUNKS), W_CHUNKS)
    ahead_expert = jnp.where(tile_expert < N_EXPERTS - 1, tile_expert + 1, N_EXPERTS - 2)
    ahead_chunk = jnp.clip(local, 0, W_CHUNKS - 1)
    flat = lambda a: a.reshape(-1).astype(jnp.int32)
    block_tables = (flat(seg), flat(dst), flat(n16))
    pad_tables = (flat(region_start + total), flat(region - total))
    tile_tables = (flat(tile_expert), flat(tiles_end[-1:]), flat(catch_from), flat(ahead_expert),
                   flat(ahead_chunk))
    return block_tables, pad_tables, tile_tables


def kernel(x_prompt, x_sample, c, cache_k_0, cache_v_0, c_ctx, ada_w_0, ada_b_0, norm1_g_0, norm2_g_0, w_qkv_0, lambda_q1_0, lambda_k1_0, lambda_q2_0, lambda_k2_0, subln_g_0, w_o_0, w_gu_0, w_down_0, ada_w_1, ada_b_1, norm1_g_1, norm2_g_1, w_fourier_1, w_router_1, w_gu_e_1, w_down_e_1, final_norm_g):
    bp, lp, _ = x_prompt.shape
    bs, ls, _ = x_sample.shape
    assert 1 + bs <= ADA_ROWS and (bp * lp) % MOE_BLOCK == 0 and ls % MOE_BLOCK == 0
    assert MOE_BLOCK % lp == 0 or lp % MOE_BLOCK == 0

    cond = jnp.zeros((ADA_ROWS, D_MODEL), F32).at[0].set(c_ctx).at[1:1 + bs].set(c)
    mod0 = _adaln(cond, ada_w_0, ada_b_0)
    mod1 = _adaln(cond, ada_w_1, ada_b_1)
    lam_params = jnp.stack([lambda_q1_0, lambda_k1_0, lambda_q2_0, lambda_k2_0])

    w_qkv = w_qkv_0.astype(BF16)
    w_o = w_o_0.astype(BF16)
    w_gu = w_gu_0.astype(BF16)
    w_down = w_down_0.astype(BF16)
    w_f = w_fourier_1.astype(BF16)
    w_router_t = w_router_1.T

    groups = [dict(x=x_prompt.reshape(bp * lp, D_MODEL), batch=bp, seq=lp, row0=0, rope=False),
              dict(x=x_sample.reshape(bs * ls, D_MODEL), batch=bs, seq=ls, row0=1, rope=True)]

    k_ctx = v_ctx = None
    for gr in groups:
        x, batch, seq, row0 = gr["x"], gr["batch"], gr["seq"], gr["row0"]
        q, k, v = _qkv(x, mod0, norm1_g_0, w_qkv, seq, row0, gr["rope"], BF16 if gr["rope"] else F32)
        if gr["rope"]:
            o = _attention(lam_params, subln_g_0, q, k, v, batch, seq, cache_k_0, cache_v_0,
                           tq=SAMPLE_Q_TILE, heads=SAMPLE_HEADS, sub=SAMPLE_Q_SUB, lag=ATTN_LAG)
        else:
            k_ctx, v_ctx = k, v
            o = _attention(lam_params, subln_g_0, q, k, v, batch, seq, tq=seq, heads=N_HEADS, sub=seq,
                           lag=ATTN_LAG)
        x, fa, fb = _post_attn(x, o, mod0, mod1, norm2_g_0, norm1_g_1, w_o, w_gu, w_down, seq, row0,
                               pair_rows=seq % (2 * MOE_BLOCK) == 0)
        gr["x"], gr["h"], gr["route"], gr["cnt"] = _fourier_router(
            x, fa, fb, mod1, w_f, norm2_g_1, w_router_t, batch, seq, row0)

    n_blocks = [gr["x"].shape[0] // MOE_BLOCK for gr in groups]
    n_pairs = 2 * sum(gr["x"].shape[0] for gr in groups)
    max_rows = n_pairs + sum(n_blocks) * N_EXPERTS * (BF16_SUBLANES - 1) + N_EXPERTS * EXPERT_TILE
    max_rows = (max_rows + EXPERT_TILE - 1) // EXPERT_TILE * EXPERT_TILE
    counts = jnp.concatenate([gr["cnt"][:, :, 0] for gr in groups], axis=0).astype(jnp.int32)
    block_tables, pad_tables, tile_tables = _moe_tables(counts, max_rows)

    xs = _dispatch(block_tables + pad_tables + tile_tables[1:2], [gr["h"] for gr in groups],
                   [gr["route"] for gr in groups], max_rows, n_pairs // EXPERT_TILE)
    ys = _experts(tile_tables, xs, w_gu_e_1, w_down_e_1)
    outs = []
    block0 = 0
    for gr, nb in zip(groups, n_blocks):
        route_t = gr["route"].transpose(0, 2, 1).reshape(-1, 8)
        outs.append(_combine(block_tables, gr["x"], route_t, mod1, final_norm_g, ys,
                             gr["seq"], gr["row0"], block0))
        block0 += nb

    y_prompt = outs[0].reshape(bp, lp, D_MODEL)
    y_sample = outs[1].reshape(bs, ls, D_MODEL)
    return (y_prompt, y_sample,
            k_ctx.reshape(bp, lp, N_HEADS, 2 * HEAD_DIM), v_ctx.reshape(bp, lp, N_HEADS, V_DIM))
```

```python
import functools
import math

import jax
import jax.numpy as jnp
import numpy as np
from jax import lax
from jax.experimental import pallas as pl
from jax.experimental.pallas import tpu as pltpu

F32 = jnp.float32
BF16 = jnp.bfloat16

D_MODEL = 1024
N_HEADS = 8
HEAD_DIM = 64
V_DIM = 2 * HEAD_DIM
GRID_W = 64
AXIS_DIM = HEAD_DIM // 2
ROPE_THETA = 10000.0
N_FOURIER_GROUPS = 4
FOURIER_GROUP = D_MODEL // N_FOURIER_GROUPS
D_FF = 2816
N_EXPERTS = 8
N_MOD = 6
EPS = 1e-6
SUBLN_EPS = 1e-5
LAMBDA_INIT_0 = 0.8 - 0.6 * math.exp(-0.3 * 0)
Q_SCALE = HEAD_DIM ** -0.5 * math.log2(math.e)

V7X_VMEM_BYTES = 64 * 1024 * 1024
VMEM_LIMIT = V7X_VMEM_BYTES - 8 * 1024 * 1024
V7X_MXU_DIM = 256
BF16_SUBLANES = 16

ADA_ROWS = 16
ADA_TN = 1536
QKV_TILE = 512
FFN_TILE = 512
FF_SPLITS = (0, 6 * V7X_MXU_DIM, D_FF)
SAMPLE_Q_TILE = 2048
SAMPLE_HEADS = 1
SAMPLE_Q_SUB = 128
ATTN_LAG = 1
MOE_BLOCK = 512
MOE_ROWS = 2 * MOE_BLOCK + 128
EXPERT_TILE = 512
W_CHUNKS = 8
SEG_BITS = (512, 256, 128, 64, 32, 16)

_NT = (((1,), (1,)), ((), ()))


def _dot(a, b):
    return jnp.dot(a, b, preferred_element_type=F32)


def _split_bf16(x):
    hi = x.astype(BF16)
    lo = (x - hi.astype(F32)).astype(BF16)
    return hi, lo


def _rms(x, g, eps):
    return x * lax.rsqrt(jnp.mean(x * x, axis=-1, keepdims=True) + eps) * g


def _cparams(n_grid, vmem=VMEM_LIMIT):
    return pltpu.CompilerParams(dimension_semantics=("arbitrary",) * n_grid, vmem_limit_bytes=vmem)


def _const_spec(shape):
    nd = len(shape)
    return pl.BlockSpec(shape, lambda *_: (0,) * nd, pipeline_mode=pl.Buffered(1))


def _adaln_kernel(c_ref, w_ref, b_ref, o_ref):
    c = c_ref[...]
    a_hi, a_lo = _split_bf16(c * jax.nn.sigmoid(c))
    w_hi, w_lo = _split_bf16(w_ref[...])
    both = _dot(jnp.concatenate([a_hi, a_lo], axis=0), w_hi)
    o_ref[...] = both[:ADA_ROWS] + both[ADA_ROWS:] + _dot(a_hi, w_lo) + b_ref[...]


def _adaln(cond, w, b):
    n = N_MOD * D_MODEL
    out = pl.pallas_call(
        _adaln_kernel,
        grid=(n // ADA_TN,),
        in_specs=[pl.BlockSpec((ADA_ROWS, D_MODEL), lambda j: (0, 0)),
                  pl.BlockSpec((D_MODEL, ADA_TN), lambda j: (0, j)),
                  pl.BlockSpec((1, ADA_TN), lambda j: (0, j))],
        out_specs=pl.BlockSpec((ADA_ROWS, ADA_TN), lambda j: (0, j)),
        out_shape=jax.ShapeDtypeStruct((ADA_ROWS, n), F32),
        compiler_params=_cparams(1),
        name="adaln",
    )(cond, w, b.reshape(1, n))
    return out.reshape(ADA_ROWS, N_MOD, D_MODEL)


def _mod_spec(seq_tiles, row0):
    if row0 == 0:
        return pl.BlockSpec((1, N_MOD, D_MODEL), lambda i, *_: (0, 0, 0))
    return pl.BlockSpec((1, N_MOD, D_MODEL), lambda i, *_: (row0 + i // seq_tiles, 0, 0))


def _qkv_kernel(*refs, rope):
    if rope:
        x_ref, mod_ref, g_ref, w_ref, cos_ref, sa_ref, sb_ref, q_ref, k_ref, v_ref = refs
    else:
        x_ref, mod_ref, g_ref, w_ref, q_ref, k_ref, v_ref = refs
    h = _rms(x_ref[...], g_ref[...], EPS)
    h = h * (1.0 + mod_ref[0, 1:2, :]) + mod_ref[0, 0:1, :]
    qkv = _dot(h.astype(BF16), w_ref[...])
    inner = N_HEADS * 2 * HEAD_DIM
    for which, ref in ((0, q_ref), (1, k_ref), (2, v_ref)):
        part = qkv[:, which * inner:(which + 1) * inner]
        if rope and which < 2:
            cos, sa, sb = cos_ref[...], sa_ref[...], sb_ref[...]
            for hd in range(N_HEADS):
                blk = part[:, hd * V_DIM:(hd + 1) * V_DIM]
                blk = (blk * cos + pltpu.roll(blk, V_DIM - AXIS_DIM // 2, 1) * sa
                       + pltpu.roll(blk, AXIS_DIM // 2, 1) * sb)
                if which == 0:
                    blk = blk * Q_SCALE
                ref[:, hd * V_DIM:(hd + 1) * V_DIM] = blk.astype(ref.dtype)
        else:
            if which == 0:
                part = part * Q_SCALE
            ref[...] = part.astype(ref.dtype)


def _rope_tables(length):
    rows = length // GRID_W
    row = jnp.repeat(jnp.arange(rows), GRID_W).astype(F32)
    col = jnp.tile(jnp.arange(GRID_W), rows).astype(F32)
    inv = 1.0 / (ROPE_THETA ** (jnp.arange(0, AXIS_DIM, 2, dtype=F32) / AXIS_DIM))
    ar = row[:, None] * inv[None, :]
    ac = col[:, None] * inv[None, :]
    ang = jnp.concatenate([ar, ar, ac, ac], axis=-1)
    cos, sin = jnp.cos(ang), jnp.sin(ang)
    first = (jnp.arange(HEAD_DIM) % AXIS_DIM) < (AXIS_DIM // 2)
    sin_a = jnp.where(first[None, :], -sin, 0.0)
    sin_b = jnp.where(first[None, :], 0.0, sin)
    wide = lambda t: jnp.concatenate([t, t], axis=-1)
    return wide(cos), wide(sin_a), wide(sin_b)


def _qkv(x2d, mod, g, w_bf16, seq_len, row0, rope, kv_dtype):
    t = x2d.shape[0]
    tm = QKV_TILE
    seq_tiles = seq_len // tm
    row_spec = pl.BlockSpec((tm, D_MODEL), lambda i: (i, 0))
    in_specs = [row_spec, _mod_spec(seq_tiles, row0), _const_spec((1, D_MODEL)),
                _const_spec((D_MODEL, 3 * D_MODEL))]
    args = [x2d, mod, g.reshape(1, D_MODEL), w_bf16]
    if rope:
        tab_spec = pl.BlockSpec((tm, V_DIM), lambda i: (i % seq_tiles, 0))
        in_specs += [tab_spec] * 3
        args += list(_rope_tables(seq_len))
    return pl.pallas_call(
        functools.partial(_qkv_kernel, rope=rope),
        grid=(t // tm,),
        in_specs=in_specs,
        out_specs=[row_spec] * 3,
        out_shape=[jax.ShapeDtypeStruct((t, D_MODEL), BF16),
                   jax.ShapeDtypeStruct((t, D_MODEL), kv_dtype),
                   jax.ShapeDtypeStruct((t, D_MODEL), kv_dtype)],
        compiler_params=_cparams(1),
        name="qkv",
    )(*args)


def _attn_kernel(*refs, has_cache, heads, sub, lag):
    if has_cache:
        lam_ref, sg_ref, q_ref, k_ref, v_ref, ck_ref, cv_ref, o_ref = refs
    else:
        lam_ref, sg_ref, q_ref, k_ref, v_ref, o_ref = refs
    lp = lam_ref[...]
    lam = (jnp.exp(jnp.sum(lp[0:1] * lp[1:2], axis=-1, keepdims=True))
           - jnp.exp(jnp.sum(lp[2:3] * lp[3:4], axis=-1, keepdims=True)) + LAMBDA_INIT_0)
    lane = lax.broadcasted_iota(jnp.int32, (1, V_DIM), 1)
    sg = sg_ref[...] * (1.0 - LAMBDA_INIT_0)
    def head_keys(hd):
        cols = slice(hd * V_DIM, (hd + 1) * V_DIM)
        k, v = k_ref[0, :, cols].astype(BF16), v_ref[0, :, cols].astype(BF16)
        if has_cache:
            k = jnp.concatenate([k, ck_ref[0, :, cols].astype(BF16)], axis=0)
            v = jnp.concatenate([v, cv_ref[0, :, cols].astype(BF16)], axis=0)
        return [(k, v)]

    chains = [(hd, r0, comp) for hd in range(heads) for r0 in range(0, q_ref.shape[0], sub)
              for comp in range(2)]
    keys = {hd: head_keys(hd) for hd in range(heads)}
    scores, probs, outs = {}, {}, {}

    def stage_scores(c):
        hd, r0, comp = chains[c]
        q = q_ref[r0:r0 + sub, hd * V_DIM:(hd + 1) * V_DIM]
        sel = (lane < HEAD_DIM) if comp == 0 else (lane >= HEAD_DIM)
        qc = jnp.where(sel, q, jnp.zeros_like(q))
        scores[c] = [lax.dot_general(qc, k, _NT, preferred_element_type=F32) for k, _ in keys[hd]]

    def stage_softmax(c):
        s = scores.pop(c)
        m = functools.reduce(jnp.maximum, [jnp.max(x, axis=-1, keepdims=True) for x in s])
        p = [jnp.exp2(x - m) for x in s]
        l = functools.reduce(jnp.add, [jnp.sum(x, axis=-1, keepdims=True) for x in p])
        probs[c] = ([x.astype(BF16) for x in p], l)

    def stage_values(c):
        hd, r0, comp = chains[c]
        p, l = probs.pop(c)
        o = functools.reduce(jnp.add, [_dot(x, v) for x, (_, v) in zip(p, keys[hd])])
        outs[c] = o * (1.0 / l)
        if comp == 1:
            o = outs.pop(c - 1) - lam * outs.pop(c)
            o_ref[r0:r0 + sub, hd * V_DIM:(hd + 1) * V_DIM] = _rms(o, sg, SUBLN_EPS).astype(o_ref.dtype)

    for t in range(len(chains) + 2 * lag):
        if t < len(chains):
            stage_scores(t)
        if 0 <= t - lag < len(chains):
            stage_softmax(t - lag)
        if 0 <= t - 2 * lag < len(chains):
            stage_values(t - 2 * lag)


def _attention(lam_params, subln_g, q, k, v, batch, seq_len, cache_k=None, cache_v=None, *,
               tq, heads, sub, lag):
    nq = seq_len // tq
    has_cache = cache_k is not None
    width = heads * V_DIM
    k3 = k.reshape(batch, seq_len, D_MODEL)
    v3 = v.reshape(batch, seq_len, D_MODEL)
    q_spec = pl.BlockSpec((tq, width), lambda b, h, i: (b * nq + i, h))
    kv_spec = pl.BlockSpec((1, seq_len, width), lambda b, h, i: (b, 0, h))
    in_specs = [_const_spec((4, HEAD_DIM)), _const_spec((1, V_DIM)), q_spec, kv_spec, kv_spec]
    args = [lam_params, subln_g.reshape(1, V_DIM), q, k3, v3]
    if has_cache:
        past = cache_k.shape[1]
        c_spec = pl.BlockSpec((1, past, width), lambda b, h, i: (b, 0, h))
        in_specs += [c_spec, c_spec]
        args += [cache_k.reshape(batch, past, D_MODEL), cache_v.reshape(batch, past, D_MODEL)]
    return pl.pallas_call(
        functools.partial(_attn_kernel, has_cache=has_cache, heads=heads, sub=sub, lag=lag),
        grid=(batch, N_HEADS // heads, nq),
        in_specs=in_specs,
        out_specs=q_spec,
        out_shape=jax.ShapeDtypeStruct((batch * seq_len, D_MODEL), BF16),
        compiler_params=_cparams(3),
        name="diff_attn",
    )(*args)


def _swiglu(h, wgu, wd):
    out = None
    for c0, c1 in zip(FF_SPLITS[:-1], FF_SPLITS[1:]):
        g = _dot(h, wgu[:, c0:c1])
        u = _dot(h, wgu[:, D_FF + c0:D_FF + c1])
        down = _dot((g * jax.nn.sigmoid(g) * u).astype(BF16), wd[c0:c1, :])
        out = down if out is None else out + down
    return out


def _dft_tables(n):
    j = np.arange(n, dtype=np.int64)
    ang = (2.0 * np.pi / n) * ((j[:, None] * j[None, :]) % n).astype(np.float64)
    s = n ** -0.5
    return np.cos(ang) * s, np.sin(ang) * s


def _post_attn_kernel(x_ref, o_ref, mod0_ref, mod1_ref, g2_ref, g1n_ref, wo_ref, wgu_ref, wd_ref,
                      cs_ref, out_ref, a_ref, b_ref, *, pair_rows):
    x1 = x_ref[...] + mod0_ref[0, 2:3, :] * _dot(o_ref[...], wo_ref[...])
    h = _rms(x1, g2_ref[...], EPS) * (1.0 + mod0_ref[0, 4:5, :]) + mod0_ref[0, 3:4, :]
    x2 = x1 + mod0_ref[0, 5:6, :] * _swiglu(h.astype(BF16), wgu_ref, wd_ref)
    out_ref[...] = x2
    h1 = _rms(x2, g1n_ref[...], EPS) * (1.0 + mod1_ref[0, 1:2, :]) + mod1_ref[0, 0:1, :]
    hb = h1.astype(BF16)
    fg = FOURIER_GROUP
    n = hb.shape[0]
    if pair_rows:
        row = lax.broadcasted_iota(jnp.int32, (n, n), 0)
        tok = lax.broadcasted_iota(jnp.int32, (n, n), 1)
        src = jnp.where(row < n // 2, 2 * row, 2 * row - (n - 1))
        hb = _dot(jnp.where(tok == src, 1.0, 0.0).astype(BF16), hb).astype(BF16)
    for grp in range(N_FOURIER_GROUPS):
        ab = _dot(hb[:, grp * fg:(grp + 1) * fg], cs_ref[...])
        cols = slice(grp * fg, (grp + 1) * fg)
        if pair_rows:
            odd_cols = slice(D_MODEL + grp * fg, D_MODEL + (grp + 1) * fg)
            a_ref[:, cols] = ab[:n // 2, :fg].astype(BF16)
            a_ref[:, odd_cols] = ab[n // 2:, :fg].astype(BF16)
            b_ref[:, cols] = ab[:n // 2, fg:].astype(BF16)
            b_ref[:, odd_cols] = ab[n // 2:, fg:].astype(BF16)
        else:
            a_ref[:, cols] = ab[:, :fg].astype(BF16)
            b_ref[:, cols] = ab[:, fg:].astype(BF16)


def _post_attn(x2d, o, mod0, mod1, g2, g1n, wo, wgu, wd, seq_len, row0, pair_rows):
    t = x2d.shape[0]
    tm = FFN_TILE if row0 == 0 else min(FFN_TILE, seq_len)
    cd, sd = _dft_tables(FOURIER_GROUP)
    cs = jnp.asarray(np.concatenate([cd, sd], axis=1), dtype=F32).astype(BF16)
    row_spec = pl.BlockSpec((tm, D_MODEL), lambda i: (i, 0))
    mod_spec = _mod_spec(seq_len // tm, row0)
    if pair_rows:
        ab_spec = pl.BlockSpec((tm // 2, 2 * D_MODEL), lambda i: (i, 0))
        ab_shape = jax.ShapeDtypeStruct((t // 2, 2 * D_MODEL), BF16)
    else:
        ab_spec, ab_shape = row_spec, jax.ShapeDtypeStruct((t, D_MODEL), BF16)
    return pl.pallas_call(
        functools.partial(_post_attn_kernel, pair_rows=pair_rows),
        grid=(t // tm,),
        in_specs=[row_spec, row_spec, mod_spec, mod_spec, _const_spec((1, D_MODEL)),
                  _const_spec((1, D_MODEL)), _const_spec((D_MODEL, D_MODEL)),
                  _const_spec((D_MODEL, 2 * D_FF)), _const_spec((D_FF, D_MODEL)),
                  _const_spec((FOURIER_GROUP, 2 * FOURIER_GROUP))],
        out_specs=[row_spec, ab_spec, ab_spec],
        out_shape=[jax.ShapeDtypeStruct((t, D_MODEL), F32), ab_shape, ab_shape],
        compiler_params=_cparams(1),
        name="post_attn_swiglu",
    )(x2d, o, mod0, mod1, g2.reshape(1, D_MODEL), g1n.reshape(1, D_MODEL), wo, wgu, wd, cs)


def _route(x, mod_ref, g_ref, wr_ref, h_ref, route_ref, cnt_ref):
    tb = MOE_BLOCK
    h = _rms(x, g_ref[...], EPS) * (1.0 + mod_ref[0, 4:5, :]) + mod_ref[0, 3:4, :]
    h_hi, h_lo = _split_bf16(h)
    h_ref[...] = h_hi
    w_hi, w_lo = _split_bf16(wr_ref[...])
    dg = lambda a, b: lax.dot_general(a, b, _NT, preferred_element_type=F32)
    logits = dg(w_hi, h_hi) + dg(w_lo, h_hi) + dg(w_hi, h_lo)
    e = jnp.exp(logits - jnp.max(logits, axis=0, keepdims=True))
    probs = e / jnp.sum(e, axis=0, keepdims=True)
    eidx = lax.broadcasted_iota(jnp.int32, (N_EXPERTS, tb), 0).astype(F32)
    big = float(N_EXPERTS)
    p1 = jnp.max(probs, axis=0, keepdims=True)
    i1 = jnp.min(jnp.where(probs == p1, eidx, big), axis=0, keepdims=True)
    oh1 = eidx == i1
    rest = jnp.where(oh1, -1.0, probs)
    p2 = jnp.max(rest, axis=0, keepdims=True)
    i2 = jnp.min(jnp.where(rest == p2, eidx, big), axis=0, keepdims=True)
    oh2 = eidx == i2
    den = p1 + p2
    oh = jnp.where(oh1 | oh2, 1.0, 0.0)
    before = (lax.broadcasted_iota(jnp.int32, (tb, tb), 0)
              < lax.broadcasted_iota(jnp.int32, (tb, tb), 1))
    rank = _dot(oh.astype(BF16), jnp.where(before, 1.0, 0.0).astype(BF16))
    cnt = jnp.sum(oh, axis=1, keepdims=True)
    cnt_ref[0] = jnp.broadcast_to(cnt, (N_EXPERTS, 128))
    n16 = jnp.floor((cnt + (BF16_SUBLANES - 1.0)) * (1.0 / BF16_SUBLANES)) * BF16_SUBLANES
    ecol = lax.broadcasted_iota(jnp.int32, (N_EXPERTS, 1), 0)
    seg = jnp.zeros_like(n16)
    for ex in range(N_EXPERTS - 1):
        seg = seg + jnp.where(ecol > ex, n16[ex:ex + 1, :], 0.0)
    pos = rank + seg
    r1 = jnp.sum(jnp.where(oh1, pos, 0.0), axis=0, keepdims=True)
    r2 = jnp.sum(jnp.where(oh2, pos, 0.0), axis=0, keepdims=True)
    zero = jnp.zeros_like(r1)
    route_ref[...] = jnp.concatenate([i1, i2, p1 / den, p2 / den, r1, r2, zero, zero], axis=0)


def _fourier_router_kernel(x_ref, mod_ref, cl_ref, sl_ref, a_ref, b_ref, wf_ref, g_ref, wr_ref,
                           out_ref, h_ref, route_ref, cnt_ref):
    ys = [_dot(cl_ref[...], a_ref[j]) + _dot(sl_ref[...], b_ref[j]) for j in range(a_ref.shape[0])]
    y = ys[0] if len(ys) == 1 else jnp.concatenate(ys, axis=0)
    x = x_ref[...] + mod_ref[0, 2:3, :] * _dot(y.astype(BF16), wf_ref[...])
    out_ref[...] = x
    _route(x, mod_ref, g_ref, wr_ref, h_ref, route_ref.at[0], cnt_ref)


def _fourier_split_router_kernel(x_ref, mod_ref, ce_ref, se_ref, co_ref, so_ref, a_ref, b_ref, wf_ref,
                                 g_ref, wr_ref, out_ref, h_ref, route_ref, cnt_ref):
    half_len = a_ref.shape[2] // 2
    even = _dot(ce_ref[...], a_ref[0, :, :half_len]) + _dot(se_ref[...], b_ref[0, :, :half_len])
    odd = _dot(co_ref[...], a_ref[0, :, half_len:]) + _dot(so_ref[...], b_ref[0, :, half_len:])
    for half, y in ((0, even + odd), (1, even - odd)):
        x = x_ref[0, half] + mod_ref[0, 2:3, :] * _dot(y.astype(BF16), wf_ref[...])
        out_ref[0, half] = x
        _route(x, mod_ref, g_ref, wr_ref, h_ref.at[0, half], route_ref.at[0, half, 0],
               cnt_ref.at[0, half])


def _fourier_split_router(x2d, a, b, mod, wf, g, wr_t, batch, seq_len, row0):
    t = x2d.shape[0]
    rows = MOE_BLOCK
    half_len = seq_len // 2
    nr = half_len // rows
    j = np.arange(half_len, dtype=np.int64)[:, None]
    m = np.arange(half_len, dtype=np.int64)[None, :]
    scale = seq_len ** -0.5

    def table(fn, k, sign):
        ang = (2.0 * np.pi / seq_len) * ((j * k) % seq_len).astype(np.float64)
        return jnp.asarray(sign * scale * fn(ang), dtype=F32).astype(BF16)

    tables = [table(np.cos, 2 * m, 1.0), table(np.sin, 2 * m, -1.0),
              table(np.cos, 2 * m + 1, 1.0), table(np.sin, 2 * m + 1, -1.0)]
    row4 = pl.BlockSpec((1, 2, rows, D_MODEL), lambda i: (i // nr, 0, i % nr, 0))
    tab_spec = pl.BlockSpec((rows, half_len), lambda i: (i % nr, 0))
    ab_spec = pl.BlockSpec((1, half_len, 2 * D_MODEL), lambda i: (i // nr, 0, 0))
    shape4 = (batch, 2, half_len, D_MODEL)
    out, h, route, cnt = pl.pallas_call(
        _fourier_split_router_kernel,
        grid=(batch * nr,),
        in_specs=[row4, _mod_spec(nr, row0)] + [tab_spec] * 4 + [ab_spec, ab_spec,
                  _const_spec((D_MODEL, D_MODEL)), _const_spec((1, D_MODEL)),
                  _const_spec((N_EXPERTS, D_MODEL))],
        out_specs=[row4, row4,
                   pl.BlockSpec((1, 2, 1, 8, rows), lambda i: (i // nr, 0, i % nr, 0, 0)),
                   pl.BlockSpec((1, 2, 1, N_EXPERTS, 128), lambda i: (i // nr, 0, i % nr, 0, 0))],
        out_shape=[jax.ShapeDtypeStruct(shape4, F32), jax.ShapeDtypeStruct(shape4, BF16),
                   jax.ShapeDtypeStruct((batch, 2, nr, 8, rows), F32),
                   jax.ShapeDtypeStruct((batch, 2, nr, N_EXPERTS, 128), F32)],
        compiler_params=_cparams(1),
        name="fourier_split_router",
    )(x2d.reshape(shape4), mod, *tables, a.reshape(batch, half_len, 2 * D_MODEL),
      b.reshape(batch, half_len, 2 * D_MODEL), wf, g.reshape(1, D_MODEL), wr_t)
    nb = t // rows
    return (out.reshape(t, D_MODEL), h.reshape(t, D_MODEL), route.reshape(nb, 8, rows),
            cnt.reshape(nb, N_EXPERTS, 128))


def _fourier_router(x2d, a, b, mod, wf, g, wr_t, batch, seq_len, row0):
    t = x2d.shape[0]
    rows = MOE_BLOCK
    if a.shape[1] == 2 * D_MODEL:
        return _fourier_split_router(x2d, a, b, mod, wf, g, wr_t, batch, seq_len, row0)
    part = min(seq_len, rows)
    nbat = rows // part
    nr = seq_len // part
    cl, sl = _dft_tables(seq_len)
    cl = jnp.asarray(cl, dtype=F32).astype(BF16)
    sl = jnp.asarray(-sl, dtype=F32).astype(BF16)
    row_spec = pl.BlockSpec((rows, D_MODEL), lambda i: (i, 0))
    tab_spec = pl.BlockSpec((part, seq_len), lambda i: (i % nr, 0))
    ab_spec = pl.BlockSpec((nbat, seq_len, D_MODEL), lambda i: (i // nr, 0, 0))
    nb = t // rows
    return pl.pallas_call(
        _fourier_router_kernel,
        grid=(nb,),
        in_specs=[row_spec, _mod_spec(nr, row0), tab_spec, tab_spec, ab_spec, ab_spec,
                  _const_spec((D_MODEL, D_MODEL)), _const_spec((1, D_MODEL)),
                  _const_spec((N_EXPERTS, D_MODEL))],
        out_specs=[row_spec, row_spec, pl.BlockSpec((1, 8, rows), lambda i: (i, 0, 0)),
                   pl.BlockSpec((1, N_EXPERTS, 128), lambda i: (i, 0, 0))],
        out_shape=[jax.ShapeDtypeStruct((t, D_MODEL), F32),
                   jax.ShapeDtypeStruct((t, D_MODEL), BF16),
                   jax.ShapeDtypeStruct((nb, 8, rows), F32),
                   jax.ShapeDtypeStruct((nb, N_EXPERTS, 128), F32)],
        compiler_params=_cparams(1),
        name="fourier_router",
    )(x2d, mod, cl, sl, a.reshape(batch, seq_len, D_MODEL), b.reshape(batch, seq_len, D_MODEL),
      wf, g.reshape(1, D_MODEL), wr_t)


def _segment_copies(n16, src_row, dst_row, make_copy):
    for bit in SEG_BITS:
        done = n16 & ~(2 * bit - 1)

        @pl.when((n16 & bit) != 0)
        def _():
            make_copy(pl.multiple_of(src_row + done, BF16_SUBLANES),
                      pl.multiple_of(dst_row + done, BF16_SUBLANES), bit)


def _dispatch_kernel(seg_ref, dst_ref, n16_ref, pad_dst_ref, pad_n_ref, nt_ref, *refs,
                     group_blocks, min_tiles):
    n_in = 2 * len(group_blocks)
    xs_ref, comp_ref, zero_ref, sem = refs[n_in:]
    i = pl.program_id(0)
    last = pl.num_programs(0) - 1
    slot = i % 2
    base = i * N_EXPERTS

    def block_copies(base_, slot_, act):
        for e in range(N_EXPERTS):
            _segment_copies(
                n16_ref[base_ + e], seg_ref[base_ + e], dst_ref[base_ + e],
                lambda s, d, n: act(pltpu.make_async_copy(
                    comp_ref.at[slot_, pl.ds(s, n)], xs_ref.at[pl.ds(d, n)], sem.at[slot_])))

    def zero_copies(act):
        for e in range(N_EXPERTS):
            _segment_copies(
                pad_n_ref[e], 0, pad_dst_ref[e],
                lambda s, d, n: act(pltpu.make_async_copy(
                    zero_ref.at[pl.ds(s, n)], xs_ref.at[pl.ds(d, n)], sem.at[2])))
        for tile in range(min_tiles, xs_ref.shape[0] // EXPERT_TILE):
            @pl.when(tile >= nt_ref[0])
            def _():
                act(pltpu.make_async_copy(
                    zero_ref, xs_ref.at[pl.ds(tile * EXPERT_TILE, EXPERT_TILE)], sem.at[2]))

    @pl.when(i == 0)
    def _():
        zero_ref[...] = jnp.zeros_like(zero_ref)
        zero_copies(lambda c: c.start())

    first = 0
    for grp, nb in enumerate(group_blocks):
        h_ref, route_ref = refs[2 * grp], refs[2 * grp + 1]

        @pl.when((i >= first) & (i < first + nb))
        def _():
            r = route_ref[0]
            pos1, pos2 = r[4:5], r[5:6]
            rows = lax.broadcasted_iota(jnp.int32, (MOE_ROWS, MOE_BLOCK), 0).astype(F32)
            onehot = jnp.where((rows == pos1) | (rows == pos2), 1.0, 0.0).astype(BF16)
            comp_ref[slot] = _dot(onehot, h_ref[...]).astype(BF16)
        first += nb
    block_copies(base, slot, lambda c: c.start())

    @pl.when(i > 0)
    def _():
        block_copies(base - N_EXPERTS, 1 - slot, lambda c: c.wait())

    @pl.when(i == last)
    def _():
        block_copies(base, slot, lambda c: c.wait())
        zero_copies(lambda c: c.wait())


def _dispatch(tables, hs, routes, n_rows, min_tiles):
    group_blocks = tuple(h.shape[0] // MOE_BLOCK for h in hs)
    in_specs, args, first = [], [], 0
    for h, route, nb in zip(hs, routes, group_blocks):
        blk = lambda i, first=first, nb=nb: jnp.clip(i - first, 0, nb - 1)
        in_specs += [pl.BlockSpec((MOE_BLOCK, D_MODEL), lambda i, *_, blk=blk: (blk(i), 0)),
                     pl.BlockSpec((1, 8, MOE_BLOCK), lambda i, *_, blk=blk: (blk(i), 0, 0))]
        args += [h, route]
        first += nb
    grid_spec = pltpu.PrefetchScalarGridSpec(
        num_scalar_prefetch=len(tables),
        grid=(sum(group_blocks),),
        in_specs=in_specs,
        out_specs=pl.BlockSpec(memory_space=pl.ANY),
        scratch_shapes=[pltpu.VMEM((2, MOE_ROWS, D_MODEL), BF16),
                        pltpu.VMEM((EXPERT_TILE, D_MODEL), BF16),
                        pltpu.SemaphoreType.DMA((3,))],
    )
    return pl.pallas_call(
        functools.partial(_dispatch_kernel, group_blocks=group_blocks, min_tiles=min_tiles),
        grid_spec=grid_spec,
        out_shape=jax.ShapeDtypeStruct((n_rows, D_MODEL), BF16),
        compiler_params=_cparams(1),
        name="moe_dispatch",
    )(*tables, *args)


def _expert_kernel(te_ref, nt_ref, catch_ref, bge_ref, bgc_ref, xs_ref, wgu_hbm, wd_hbm, ys_ref,
                   wgu_bf, wd_bf, stage_gu, stage_d, sem):
    i = pl.program_id(0)
    used = i < nt_ref[0]
    rg, rd = D_MODEL // W_CHUNKS, D_FF // W_CHUNKS

    def chunk_copies(expert, c, slot):
        return (pltpu.make_async_copy(wgu_hbm.at[expert, pl.ds(pl.multiple_of(c * rg, rg), rg)],
                                      stage_gu.at[slot], sem.at[0, slot]),
                pltpu.make_async_copy(wd_hbm.at[expert, pl.ds(pl.multiple_of(c * rd, rd), rd)],
                                      stage_d.at[slot], sem.at[1, slot]))

    def cast_chunk(expert, c, slot):
        ws = expert % 2
        wgu_bf[ws, pl.ds(pl.multiple_of(c * rg, rg), rg), :] = stage_gu[slot].astype(BF16)
        wd_bf[ws, pl.ds(pl.multiple_of(c * rd, rd), rd), :] = stage_d[slot].astype(BF16)

    @pl.when(used)
    def _():
        e = te_ref[i]

        @pl.when(i == 0)
        def _():
            for cp in chunk_copies(bge_ref[0], bgc_ref[0], 0):
                cp.start()

        @pl.when(i + 1 < nt_ref[0])
        def _():
            for cp in chunk_copies(bge_ref[i + 1], bgc_ref[i + 1], (i + 1) % 2):
                cp.start()

        def catch_up(c, carry):
            cps = chunk_copies(e, c, 2)
            for cp in cps:
                cp.start()
            for cp in cps:
                cp.wait()
            cast_chunk(e, c, 2)
            return carry

        lax.fori_loop(catch_ref[i], W_CHUNKS, catch_up, 0)

        for cp in chunk_copies(bge_ref[i], bgc_ref[i], i % 2):
            cp.wait()
        cast_chunk(bge_ref[i], bgc_ref[i], i % 2)
        ws = e % 2
        ys_ref[...] = _swiglu(xs_ref[...], wgu_bf.at[ws], wd_bf.at[ws]).astype(ys_ref.dtype)

    @pl.when(jnp.logical_not(used))
    def _():
        ys_ref[...] = jnp.zeros_like(ys_ref)


def _experts(tile_tables, xs, wgu_e, wd_e):
    rows = xs.shape[0]
    tm = EXPERT_TILE
    rg, rd = D_MODEL // W_CHUNKS, D_FF // W_CHUNKS
    grid_spec = pltpu.PrefetchScalarGridSpec(
        num_scalar_prefetch=len(tile_tables),
        grid=(rows // tm,),
        in_specs=[pl.BlockSpec((tm, D_MODEL), lambda i, te, nt, *_: (jnp.minimum(i, nt[0] - 1), 0)),
                  pl.BlockSpec(memory_space=pl.ANY), pl.BlockSpec(memory_space=pl.ANY)],
        out_specs=pl.BlockSpec((tm, D_MODEL), lambda i, *_: (i, 0)),
        scratch_shapes=[pltpu.VMEM((2, D_MODEL, 2 * D_FF), BF16), pltpu.VMEM((2, D_FF, D_MODEL), BF16),
                        pltpu.VMEM((3, rg, 2 * D_FF), F32), pltpu.VMEM((3, rd, D_MODEL), F32),
                        pltpu.SemaphoreType.DMA((2, 3))],
    )
    return pl.pallas_call(
        _expert_kernel,
        grid_spec=grid_spec,
        out_shape=jax.ShapeDtypeStruct((rows, D_MODEL), BF16),
        compiler_params=_cparams(1),
        name="moe_experts",
    )(*tile_tables, xs, wgu_e, wd_e)


def _combine_kernel(seg_ref, dst_ref, n16_ref, x_ref, rt_ref, mod_ref, g_ref, ys_ref, out_ref,
                    buf_ref, sem, *, block0):
    i = pl.program_id(0)
    slot = i % 2
    base = (block0 + i) * N_EXPERTS

    def block_copies(base_, slot_, act):
        for e in range(N_EXPERTS):
            _segment_copies(
                n16_ref[base_ + e], seg_ref[base_ + e], dst_ref[base_ + e],
                lambda s, d, n: act(pltpu.make_async_copy(
                    ys_ref.at[pl.ds(d, n)], buf_ref.at[slot_, pl.ds(s, n)], sem.at[slot_])))

    def fetch(base_, slot_):
        buf_ref[slot_] = jnp.zeros(buf_ref.shape[1:], buf_ref.dtype)
        block_copies(base_, slot_, lambda c: c.start())

    @pl.when(i == 0)
    def _():
        fetch(base, slot)

    @pl.when(i + 1 < pl.num_programs(0))
    def _():
        fetch(base + N_EXPERTS, 1 - slot)

    rt = rt_ref[...]
    pos1, pos2 = rt[:, 4:5], rt[:, 5:6]
    cols = lax.broadcasted_iota(jnp.int32, (MOE_BLOCK, MOE_ROWS), 1).astype(F32)
    gates = (jnp.where(cols == pos1, rt[:, 2:3], 0.0)
             + jnp.where(cols == pos2, rt[:, 3:4], 0.0)).astype(BF16)
    block_copies(base, slot, lambda c: c.wait())
    x = x_ref[...] + mod_ref[0, 5:6, :] * _dot(gates, buf_ref[slot])
    out_ref[...] = _rms(x, g_ref[...], EPS)


def _combine(tables, x2d, route_t, mod, g, ys, seq_len, row0, block0):
    t = x2d.shape[0]
    tb = MOE_BLOCK
    grid_spec = pltpu.PrefetchScalarGridSpec(
        num_scalar_prefetch=len(tables),
        grid=(t // tb,),
        in_specs=[pl.BlockSpec((tb, D_MODEL), lambda i, *_: (i, 0)),
                  pl.BlockSpec((tb, 8), lambda i, *_: (i, 0)),
                  _mod_spec(max(seq_len // tb, 1), row0),
                  pl.BlockSpec((1, D_MODEL), lambda i, *_: (0, 0)),
                  pl.BlockSpec(memory_space=pl.ANY)],
        out_specs=pl.BlockSpec((tb, D_MODEL), lambda i, *_: (i, 0)),
        scratch_shapes=[pltpu.VMEM((2, MOE_ROWS, D_MODEL), BF16), pltpu.SemaphoreType.DMA((2,))],
    )
    return pl.pallas_call(
        functools.partial(_combine_kernel, block0=block0),
        grid_spec=grid_spec,
        out_shape=jax.ShapeDtypeStruct((t, D_MODEL), F32),
        compiler_params=_cparams(1),
        name="moe_combine",
    )(*tables, x2d, route_t, mod, g.reshape(1, D_MODEL), ys)


def _moe_tables(counts, n_rows):
    pad = BF16_SUBLANES
    n16 = (counts + pad - 1) // pad * pad
    seg = jnp.cumsum(n16, axis=1) - n16
    total = jnp.sum(n16, axis=0)
    region = (total + EXPERT_TILE - 1) // EXPERT_TILE * EXPERT_TILE
    region_end = jnp.cumsum(region)
    region_start = region_end - region
    dst = region_start[None, :] + jnp.cumsum(n16, axis=0) - n16
    tiles_end = region_end // EXPERT_TILE
    tile_ids = jnp.arange(n_rows // EXPERT_TILE, dtype=jnp.int32)
    tile_expert = jnp.minimum(jnp.sum(tile_ids[:, None] >= tiles_end[None, :], axis=1), N_EXPERTS - 1)
    n_tiles_e = region // EXPERT_TILE
    local = tile_ids - (tiles_end - n_tiles_e)[tile_expert]
    prev_tiles = jnp.concatenate([jnp.zeros((1,), n_tiles_e.dtype), n_tiles_e[:-1]])
    catch_from = jnp.where(local == 0, jnp.minimum(prev_tiles[tile_expert], W_CHUNKS), W_CHUNKS)
    ahead_expert = jnp.where(tile_expert < N_EXPERTS - 1, tile_expert + 1, N_EXPERTS - 2)
    ahead_chunk = jnp.clip(local, 0, W_CHUNKS - 1)
    flat = lambda a: a.reshape(-1).astype(jnp.int32)
    block_tables = (flat(seg), flat(dst), flat(n16))
    pad_tables = (flat(region_start + total), flat(region - total))
    tile_tables = (flat(tile_expert), flat(tiles_end[-1:]), flat(catch_from), flat(ahead_expert),
                   flat(ahead_chunk))
    return block_tables, pad_tables, tile_tables


def kernel(x_prompt, x_sample, c, cache_k_0, cache_v_0, c_ctx, ada_w_0, ada_b_0, norm1_g_0, norm2_g_0, w_qkv_0, lambda_q1_0, lambda_k1_0, lambda_q2_0, lambda_k2_0, subln_g_0, w_o_0, w_gu_0, w_down_0, ada_w_1, ada_b_1, norm1_g_1, norm2_g_1, w_fourier_1, w_router_1, w_gu_e_1, w_down_e_1, final_norm_g):
    bp, lp, _ = x_prompt.shape
    bs, ls, _ = x_sample.shape
    assert 1 + bs <= ADA_ROWS and (bp * lp) % MOE_BLOCK == 0 and ls % MOE_BLOCK == 0
    assert MOE_BLOCK % lp == 0 or lp % MOE_BLOCK == 0

    cond = jnp.zeros((ADA_ROWS, D_MODEL), F32).at[0].set(c_ctx).at[1:1 + bs].set(c)
    mod0 = _adaln(cond, ada_w_0, ada_b_0)
    mod1 = _adaln(cond, ada_w_1, ada_b_1)
    lam_params = jnp.stack([lambda_q1_0, lambda_k1_0, lambda_q2_0, lambda_k2_0])

    w_qkv = w_qkv_0.astype(BF16)
    w_o = w_o_0.astype(BF16)
    w_gu = w_gu_0.astype(BF16)
    w_down = w_down_0.astype(BF16)
    w_f = w_fourier_1.astype(BF16)
    w_router_t = w_router_1.T

    groups = [dict(x=x_prompt.reshape(bp * lp, D_MODEL), batch=bp, seq=lp, row0=0, rope=False),
              dict(x=x_sample.reshape(bs * ls, D_MODEL), batch=bs, seq=ls, row0=1, rope=True)]

    k_ctx = v_ctx = None
    for gr in groups:
        x, batch, seq, row0 = gr["x"], gr["batch"], gr["seq"], gr["row0"]
        q, k, v = _qkv(x, mod0, norm1_g_0, w_qkv, seq, row0, gr["rope"], BF16 if gr["rope"] else F32)
        if gr["rope"]:
            o = _attention(lam_params, subln_g_0, q, k, v, batch, seq, cache_k_0, cache_v_0,
                           tq=SAMPLE_Q_TILE, heads=SAMPLE_HEADS, sub=SAMPLE_Q_SUB, lag=ATTN_LAG)
        else:
            k_ctx, v_ctx = k, v
            o = _attention(lam_params, subln_g_0, q, k, v, batch, seq, tq=seq, heads=N_HEADS, sub=seq,
                           lag=ATTN_LAG)
        x, fa, fb = _post_attn(x, o, mod0, mod1, norm2_g_0, norm1_g_1, w_o, w_gu, w_down, seq, row0,
                               pair_rows=seq % (2 * MOE_BLOCK) == 0)
        gr["x"], gr["h"], gr["route"], gr["cnt"] = _fourier_router(
            x, fa, fb, mod1, w_f, norm2_g_1, w_router_t, batch, seq, row0)

    n_blocks = [gr["x"].shape[0] // MOE_BLOCK for gr in groups]
    n_pairs = 2 * sum(gr["x"].shape[0] for gr in groups)
    max_rows = n_pairs + sum(n_blocks) * N_EXPERTS * (BF16_SUBLANES - 1) + N_EXPERTS * EXPERT_TILE
    max_rows = (max_rows + EXPERT_TILE - 1) // EXPERT_TILE * EXPERT_TILE
    counts = jnp.concatenate([gr["cnt"][:, :, 0] for gr in groups], axis=0).astype(jnp.int32)
    block_tables, pad_tables, tile_tables = _moe_tables(counts, max_rows)

    xs = _dispatch(block_tables + pad_tables + tile_tables[1:2], [gr["h"] for gr in groups],
                   [gr["route"] for gr in groups], max_rows, n_pairs // EXPERT_TILE)
    ys = _experts(tile_tables, xs, w_gu_e_1, w_down_e_1)
    outs = []
    block0 = 0
    for gr, nb in zip(groups, n_blocks):
        route_t = gr["route"].transpose(0, 2, 1).reshape(-1, 8)
        outs.append(_combine(block_tables, gr["x"], route_t, mod1, final_norm_g, ys,
                             gr["seq"], gr["row0"], block0))
        block0 += nb

    y_prompt = outs[0].reshape(bp, lp, D_MODEL)
    y_sample = outs[1].reshape(bs, ls, D_MODEL)
    return (y_prompt, y_sample,
            k_ctx.reshape(bp, lp, N_HEADS, 2 * HEAD_DIM), v_ctx.reshape(bp, lp, N_HEADS, V_DIM))
```

```python
import functools
import math

import jax
import jax.numpy as jnp
import numpy as np
from jax import lax
from jax.experimental import pallas as pl
from jax.experimental.pallas import tpu as pltpu

F32 = jnp.float32
BF16 = jnp.bfloat16

D_MODEL = 1024
N_HEADS = 8
HEAD_DIM = 64
V_DIM = 2 * HEAD_DIM
GRID_W = 64
AXIS_DIM = HEAD_DIM // 2
ROPE_THETA = 10000.0
N_FOURIER_GROUPS = 4
FOURIER_GROUP = D_MODEL // N_FOURIER_GROUPS
D_FF = 2816
N_EXPERTS = 8
N_MOD = 6
EPS = 1e-6
SUBLN_EPS = 1e-5
LAMBDA_INIT_0 = 0.8 - 0.6 * math.exp(-0.3 * 0)
Q_SCALE = HEAD_DIM ** -0.5 * math.log2(math.e)

V7X_VMEM_BYTES = 64 * 1024 * 1024
VMEM_LIMIT = V7X_VMEM_BYTES - 8 * 1024 * 1024
V7X_MXU_DIM = 256
BF16_SUBLANES = 16

ADA_ROWS = 16
ADA_TN = 1536
QKV_TILE = 512
FFN_TILE = 512
FF_SPLITS = (0, 6 * V7X_MXU_DIM, D_FF)
SAMPLE_Q_TILE = 2048
SAMPLE_HEADS = 1
SAMPLE_Q_SUB = 128
ATTN_LAG = 1
MOE_BLOCK = 512
MOE_ROWS = 2 * MOE_BLOCK + 128
EXPERT_TILE = 512
W_CHUNKS = 8

_NT = (((1,), (1,)), ((), ()))


def _dot(a, b):
    return jnp.dot(a, b, preferred_element_type=F32)


def _split_bf16(x):
    hi = x.astype(BF16)
    lo = (x - hi.astype(F32)).astype(BF16)
    return hi, lo


def _rms(x, g, eps):
    return x * lax.rsqrt(jnp.mean(x * x, axis=-1, keepdims=True) + eps) * g


def _cparams(n_grid, vmem=VMEM_LIMIT):
    return pltpu.CompilerParams(dimension_semantics=("arbitrary",) * n_grid, vmem_limit_bytes=vmem)


def _const_spec(shape):
    nd = len(shape)
    return pl.BlockSpec(shape, lambda *_: (0,) * nd, pipeline_mode=pl.Buffered(1))


def _adaln_kernel(c_ref, w_ref, b_ref, o_ref):
    c = c_ref[...]
    a_hi, a_lo = _split_bf16(c * jax.nn.sigmoid(c))
    w_hi, w_lo = _split_bf16(w_ref[...])
    both = _dot(jnp.concatenate([a_hi, a_lo], axis=0), w_hi)
    o_ref[...] = both[:ADA_ROWS] + both[ADA_ROWS:] + _dot(a_hi, w_lo) + b_ref[...]


def _adaln(cond, w, b):
    n = N_MOD * D_MODEL
    out = pl.pallas_call(
        _adaln_kernel,
        grid=(n // ADA_TN,),
        in_specs=[pl.BlockSpec((ADA_ROWS, D_MODEL), lambda j: (0, 0)),
                  pl.BlockSpec((D_MODEL, ADA_TN), lambda j: (0, j)),
                  pl.BlockSpec((1, ADA_TN), lambda j: (0, j))],
        out_specs=pl.BlockSpec((ADA_ROWS, ADA_TN), lambda j: (0, j)),
        out_shape=jax.ShapeDtypeStruct((ADA_ROWS, n), F32),
        compiler_params=_cparams(1),
        name="adaln",
    )(cond, w, b.reshape(1, n))
    return out.reshape(ADA_ROWS, N_MOD, D_MODEL)


def _mod_spec(seq_tiles, row0):
    if row0 == 0:
        return pl.BlockSpec((1, N_MOD, D_MODEL), lambda i, *_: (0, 0, 0))
    return pl.BlockSpec((1, N_MOD, D_MODEL), lambda i, *_: (row0 + i // seq_tiles, 0, 0))


def _qkv_kernel(*refs, rope):
    if rope:
        x_ref, mod_ref, g_ref, w_ref, cos_ref, sa_ref, sb_ref, q_ref, k_ref, v_ref = refs
    else:
        x_ref, mod_ref, g_ref, w_ref, q_ref, k_ref, v_ref = refs
    h = _rms(x_ref[...], g_ref[...], EPS)
    h = h * (1.0 + mod_ref[0, 1:2, :]) + mod_ref[0, 0:1, :]
    qkv = _dot(h.astype(BF16), w_ref[...])
    inner = N_HEADS * 2 * HEAD_DIM
    for which, ref in ((0, q_ref), (1, k_ref), (2, v_ref)):
        part = qkv[:, which * inner:(which + 1) * inner]
        if rope and which < 2:
            cos, sa, sb = cos_ref[...], sa_ref[...], sb_ref[...]
            for hd in range(N_HEADS):
                blk = part[:, hd * V_DIM:(hd + 1) * V_DIM]
                blk = (blk * cos + pltpu.roll(blk, V_DIM - AXIS_DIM // 2, 1) * sa
                       + pltpu.roll(blk, AXIS_DIM // 2, 1) * sb)
                if which == 0:
                    blk = blk * Q_SCALE
                ref[:, hd * V_DIM:(hd + 1) * V_DIM] = blk.astype(ref.dtype)
        else:
            if which == 0:
                part = part * Q_SCALE
            ref[...] = part.astype(ref.dtype)


def _rope_tables(length):
    rows = length // GRID_W
    row = jnp.repeat(jnp.arange(rows), GRID_W).astype(F32)
    col = jnp.tile(jnp.arange(GRID_W), rows).astype(F32)
    inv = 1.0 / (ROPE_THETA ** (jnp.arange(0, AXIS_DIM, 2, dtype=F32) / AXIS_DIM))
    ar = row[:, None] * inv[None, :]
    ac = col[:, None] * inv[None, :]
    ang = jnp.concatenate([ar, ar, ac, ac], axis=-1)
    cos, sin = jnp.cos(ang), jnp.sin(ang)
    first = (jnp.arange(HEAD_DIM) % AXIS_DIM) < (AXIS_DIM // 2)
    sin_a = jnp.where(first[None, :], -sin, 0.0)
    sin_b = jnp.where(first[None, :], 0.0, sin)
    wide = lambda t: jnp.concatenate([t, t], axis=-1)
    return wide(cos), wide(sin_a), wide(sin_b)


def _qkv(x2d, mod, g, w_bf16, seq_len, row0, rope, kv_dtype):
    t = x2d.shape[0]
    tm = QKV_TILE
    seq_tiles = seq_len // tm
    row_spec = pl.BlockSpec((tm, D_MODEL), lambda i: (i, 0))
    in_specs = [row_spec, _mod_spec(seq_tiles, row0), _const_spec((1, D_MODEL)),
                _const_spec((D_MODEL, 3 * D_MODEL))]
    args = [x2d, mod, g.reshape(1, D_MODEL), w_bf16]
    if rope:
        tab_spec = pl.BlockSpec((tm, V_DIM), lambda i: (i % seq_tiles, 0))
        in_specs += [tab_spec] * 3
        args += list(_rope_tables(seq_len))
    return pl.pallas_call(
        functools.partial(_qkv_kernel, rope=rope),
        grid=(t // tm,),
        in_specs=in_specs,
        out_specs=[row_spec] * 3,
        out_shape=[jax.ShapeDtypeStruct((t, D_MODEL), BF16),
                   jax.ShapeDtypeStruct((t, D_MODEL), kv_dtype),
                   jax.ShapeDtypeStruct((t, D_MODEL), kv_dtype)],
        compiler_params=_cparams(1),
        name="qkv",
    )(*args)


def _attn_kernel(*refs, has_cache, heads, sub, lag):
    if has_cache:
        lam_ref, sg_ref, q_ref, k_ref, v_ref, ck_ref, cv_ref, o_ref = refs
    else:
        lam_ref, sg_ref, q_ref, k_ref, v_ref, o_ref = refs
    lp = lam_ref[...]
    lam = (jnp.exp(jnp.sum(lp[0:1] * lp[1:2], axis=-1, keepdims=True))
           - jnp.exp(jnp.sum(lp[2:3] * lp[3:4], axis=-1, keepdims=True)) + LAMBDA_INIT_0)
    lane = lax.broadcasted_iota(jnp.int32, (1, V_DIM), 1)
    sg = sg_ref[...] * (1.0 - LAMBDA_INIT_0)
    def head_keys(hd):
        cols = slice(hd * V_DIM, (hd + 1) * V_DIM)
        k, v = k_ref[0, :, cols].astype(BF16), v_ref[0, :, cols].astype(BF16)
        if has_cache:
            k = jnp.concatenate([k, ck_ref[0, :, cols].astype(BF16)], axis=0)
            v = jnp.concatenate([v, cv_ref[0, :, cols].astype(BF16)], axis=0)
        return [(k, v)]

    chains = [(hd, r0, comp) for hd in range(heads) for r0 in range(0, q_ref.shape[0], sub)
              for comp in range(2)]
    keys = {hd: head_keys(hd) for hd in range(heads)}
    scores, probs, outs = {}, {}, {}

    def stage_scores(c):
        hd, r0, comp = chains[c]
        q = q_ref[r0:r0 + sub, hd * V_DIM:(hd + 1) * V_DIM]
        sel = (lane < HEAD_DIM) if comp == 0 else (lane >= HEAD_DIM)
        qc = jnp.where(sel, q, jnp.zeros_like(q))
        scores[c] = [lax.dot_general(qc, k, _NT, preferred_element_type=F32) for k, _ in keys[hd]]

    def stage_softmax(c):
        s = scores.pop(c)
        m = functools.reduce(jnp.maximum, [jnp.max(x, axis=-1, keepdims=True) for x in s])
        p = [jnp.exp2(x - m) for x in s]
        l = functools.reduce(jnp.add, [jnp.sum(x, axis=-1, keepdims=True) for x in p])
        probs[c] = ([x.astype(BF16) for x in p], l)

    def stage_values(c):
        hd, r0, comp = chains[c]
        p, l = probs.pop(c)
        o = functools.reduce(jnp.add, [_dot(x, v) for x, (_, v) in zip(p, keys[hd])])
        outs[c] = o * (1.0 / l)
        if comp == 1:
            o = outs.pop(c - 1) - lam * outs.pop(c)
            o_ref[r0:r0 + sub, hd * V_DIM:(hd + 1) * V_DIM] = _rms(o, sg, SUBLN_EPS).astype(o_ref.dtype)

    for t in range(len(chains) + 2 * lag):
        if t < len(chains):
            stage_scores(t)
        if 0 <= t - lag < len(chains):
            stage_softmax(t - lag)
        if 0 <= t - 2 * lag < len(chains):
            stage_values(t - 2 * lag)


def _attention(lam_params, subln_g, q, k, v, batch, seq_len, cache_k=None, cache_v=None, *,
               tq, heads, sub, lag):
    nq = seq_len // tq
    has_cache = cache_k is not None
    width = heads * V_DIM
    k3 = k.reshape(batch, seq_len, D_MODEL)
    v3 = v.reshape(batch, seq_len, D_MODEL)
    q_spec = pl.BlockSpec((tq, width), lambda b, h, i: (b * nq + i, h))
    kv_spec = pl.BlockSpec((1, seq_len, width), lambda b, h, i: (b, 0, h))
    in_specs = [_const_spec((4, HEAD_DIM)), _const_spec((1, V_DIM)), q_spec, kv_spec, kv_spec]
    args = [lam_params, subln_g.reshape(1, V_DIM), q, k3, v3]
    if has_cache:
        past = cache_k.shape[1]
        c_spec = pl.BlockSpec((1, past, width), lambda b, h, i: (b, 0, h))
        in_specs += [c_spec, c_spec]
        args += [cache_k.reshape(batch, past, D_MODEL), cache_v.reshape(batch, past, D_MODEL)]
    return pl.pallas_call(
        functools.partial(_attn_kernel, has_cache=has_cache, heads=heads, sub=sub, lag=lag),
        grid=(batch, N_HEADS // heads, nq),
        in_specs=in_specs,
        out_specs=q_spec,
        out_shape=jax.ShapeDtypeStruct((batch * seq_len, D_MODEL), BF16),
        compiler_params=_cparams(3),
        name="diff_attn",
    )(*args)


def _swiglu(h, wgu, wd):
    out = None
    for c0, c1 in zip(FF_SPLITS[:-1], FF_SPLITS[1:]):
        g = _dot(h, wgu[:, c0:c1])
        u = _dot(h, wgu[:, D_FF + c0:D_FF + c1])
        down = _dot((g * jax.nn.sigmoid(g) * u).astype(BF16), wd[c0:c1, :])
        out = down if out is None else out + down
    return out


def _dft_tables(n):
    j = np.arange(n, dtype=np.int64)
    ang = (2.0 * np.pi / n) * ((j[:, None] * j[None, :]) % n).astype(np.float64)
    s = n ** -0.5
    return np.cos(ang) * s, np.sin(ang) * s


def _post_attn_kernel(x_ref, o_ref, mod0_ref, mod1_ref, g2_ref, g1n_ref, wo_ref, wgu_ref, wd_ref,
                      cs_ref, out_ref, a_ref, b_ref, *, pair_rows):
    x1 = x_ref[...] + mod0_ref[0, 2:3, :] * _dot(o_ref[...], wo_ref[...])
    h = _rms(x1, g2_ref[...], EPS) * (1.0 + mod0_ref[0, 4:5, :]) + mod0_ref[0, 3:4, :]
    x2 = x1 + mod0_ref[0, 5:6, :] * _swiglu(h.astype(BF16), wgu_ref, wd_ref)
    out_ref[...] = x2
    h1 = _rms(x2, g1n_ref[...], EPS) * (1.0 + mod1_ref[0, 1:2, :]) + mod1_ref[0, 0:1, :]
    hb = h1.astype(BF16)
    fg = FOURIER_GROUP
    n = hb.shape[0]
    if pair_rows:
        row = lax.broadcasted_iota(jnp.int32, (n, n), 0)
        tok = lax.broadcasted_iota(jnp.int32, (n, n), 1)
        src = jnp.where(row < n // 2, 2 * row, 2 * row - (n - 1))
        hb = _dot(jnp.where(tok == src, 1.0, 0.0).astype(BF16), hb).astype(BF16)
    for grp in range(N_FOURIER_GROUPS):
        ab = _dot(hb[:, grp * fg:(grp + 1) * fg], cs_ref[...])
        cols = slice(grp * fg, (grp + 1) * fg)
        if pair_rows:
            odd_cols = slice(D_MODEL + grp * fg, D_MODEL + (grp + 1) * fg)
            a_ref[:, cols] = ab[:n // 2, :fg].astype(BF16)
            a_ref[:, odd_cols] = ab[n // 2:, :fg].astype(BF16)
            b_ref[:, cols] = ab[:n // 2, fg:].astype(BF16)
            b_ref[:, odd_cols] = ab[n // 2:, fg:].astype(BF16)
        else:
            a_ref[:, cols] = ab[:, :fg].astype(BF16)
            b_ref[:, cols] = ab[:, fg:].astype(BF16)


def _post_attn(x2d, o, mod0, mod1, g2, g1n, wo, wgu, wd, seq_len, row0, pair_rows):
    t = x2d.shape[0]
    tm = FFN_TILE if row0 == 0 else min(FFN_TILE, seq_len)
    cd, sd = _dft_tables(FOURIER_GROUP)
    cs = jnp.asarray(np.concatenate([cd, sd], axis=1), dtype=F32).astype(BF16)
    row_spec = pl.BlockSpec((tm, D_MODEL), lambda i: (i, 0))
    mod_spec = _mod_spec(seq_len // tm, row0)
    if pair_rows:
        ab_spec = pl.BlockSpec((tm // 2, 2 * D_MODEL), lambda i: (i, 0))
        ab_shape = jax.ShapeDtypeStruct((t // 2, 2 * D_MODEL), BF16)
    else:
        ab_spec, ab_shape = row_spec, jax.ShapeDtypeStruct((t, D_MODEL), BF16)
    return pl.pallas_call(
        functools.partial(_post_attn_kernel, pair_rows=pair_rows),
        grid=(t // tm,),
        in_specs=[row_spec, row_spec, mod_spec, mod_spec, _const_spec((1, D_MODEL)),
                  _const_spec((1, D_MODEL)), _const_spec((D_MODEL, D_MODEL)),
                  _const_spec((D_MODEL, 2 * D_FF)), _const_spec((D_FF, D_MODEL)),
                  _const_spec((FOURIER_GROUP, 2 * FOURIER_GROUP))],
        out_specs=[row_spec, ab_spec, ab_spec],
        out_shape=[jax.ShapeDtypeStruct((t, D_MODEL), F32), ab_shape, ab_shape],
        compiler_params=_cparams(1),
        name="post_attn_swiglu",
    )(x2d, o, mod0, mod1, g2.reshape(1, D_MODEL), g1n.reshape(1, D_MODEL), wo, wgu, wd, cs)


def _route(x, mod_ref, g_ref, wr_ref, h_ref, route_ref, cnt_ref):
    tb = MOE_BLOCK
    h = _rms(x, g_ref[...], EPS) * (1.0 + mod_ref[0, 4:5, :]) + mod_ref[0, 3:4, :]
    h_hi, h_lo = _split_bf16(h)
    h_ref[...] = h_hi
    w_hi, w_lo = _split_bf16(wr_ref[...])
    dg = lambda a, b: lax.dot_general(a, b, _NT, preferred_element_type=F32)
    logits = dg(w_hi, h_hi) + dg(w_lo, h_hi) + dg(w_hi, h_lo)
    e = jnp.exp(logits - jnp.max(logits, axis=0, keepdims=True))
    probs = e / jnp.sum(e, axis=0, keepdims=True)
    eidx = lax.broadcasted_iota(jnp.int32, (N_EXPERTS, tb), 0).astype(F32)
    big = float(N_EXPERTS)
    p1 = jnp.max(probs, axis=0, keepdims=True)
    i1 = jnp.min(jnp.where(probs == p1, eidx, big), axis=0, keepdims=True)
    oh1 = eidx == i1
    rest = jnp.where(oh1, -1.0, probs)
    p2 = jnp.max(rest, axis=0, keepdims=True)
    i2 = jnp.min(jnp.where(rest == p2, eidx, big), axis=0, keepdims=True)
    oh2 = eidx == i2
    den = p1 + p2
    oh = jnp.where(oh1 | oh2, 1.0, 0.0)
    before = (lax.broadcasted_iota(jnp.int32, (tb, tb), 0)
              < lax.broadcasted_iota(jnp.int32, (tb, tb), 1))
    rank = _dot(oh.astype(BF16), jnp.where(before, 1.0, 0.0).astype(BF16))
    cnt = jnp.sum(oh, axis=1, keepdims=True)
    cnt_ref[0] = jnp.broadcast_to(cnt, (N_EXPERTS, 128))
    n16 = jnp.floor((cnt + (BF16_SUBLANES - 1.0)) * (1.0 / BF16_SUBLANES)) * BF16_SUBLANES
    ecol = lax.broadcasted_iota(jnp.int32, (N_EXPERTS, 1), 0)
    seg = jnp.zeros_like(n16)
    for ex in range(N_EXPERTS - 1):
        seg = seg + jnp.where(ecol > ex, n16[ex:ex + 1, :], 0.0)
    pos = rank + seg
    r1 = jnp.sum(jnp.where(oh1, pos, 0.0), axis=0, keepdims=True)
    r2 = jnp.sum(jnp.where(oh2, pos, 0.0), axis=0, keepdims=True)
    zero = jnp.zeros_like(r1)
    route_ref[...] = jnp.concatenate([i1, i2, p1 / den, p2 / den, r1, r2, zero, zero], axis=0)


def _fourier_router_kernel(x_ref, mod_ref, cl_ref, sl_ref, a_ref, b_ref, wf_ref, g_ref, wr_ref,
                           out_ref, h_ref, route_ref, cnt_ref):
    ys = [_dot(cl_ref[...], a_ref[j]) + _dot(sl_ref[...], b_ref[j]) for j in range(a_ref.shape[0])]
    y = ys[0] if len(ys) == 1 else jnp.concatenate(ys, axis=0)
    x = x_ref[...] + mod_ref[0, 2:3, :] * _dot(y.astype(BF16), wf_ref[...])
    out_ref[...] = x
    _route(x, mod_ref, g_ref, wr_ref, h_ref, route_ref.at[0], cnt_ref)


def _fourier_split_router_kernel(x_ref, mod_ref, ce_ref, se_ref, co_ref, so_ref, a_ref, b_ref, wf_ref,
                                 g_ref, wr_ref, out_ref, h_ref, route_ref, cnt_ref):
    half_len = a_ref.shape[2] // 2
    even = _dot(ce_ref[...], a_ref[0, :, :half_len]) + _dot(se_ref[...], b_ref[0, :, :half_len])
    odd = _dot(co_ref[...], a_ref[0, :, half_len:]) + _dot(so_ref[...], b_ref[0, :, half_len:])
    xs = []
    for half, y in ((0, even + odd), (1, even - odd)):
        x = x_ref[0, half] + mod_ref[0, 2:3, :] * _dot(y.astype(BF16), wf_ref[...])
        out_ref[0, half] = x
        xs.append(x)
    for half, x in enumerate(xs):
        _route(x, mod_ref, g_ref, wr_ref, h_ref.at[0, half], route_ref.at[0, half, 0],
               cnt_ref.at[0, half])


def _fourier_split_router(x2d, a, b, mod, wf, g, wr_t, batch, seq_len, row0):
    t = x2d.shape[0]
    rows = MOE_BLOCK
    half_len = seq_len // 2
    nr = half_len // rows
    j = np.arange(half_len, dtype=np.int64)[:, None]
    m = np.arange(half_len, dtype=np.int64)[None, :]
    scale = seq_len ** -0.5

    def table(fn, k, sign):
        ang = (2.0 * np.pi / seq_len) * ((j * k) % seq_len).astype(np.float64)
        return jnp.asarray(sign * scale * fn(ang), dtype=F32).astype(BF16)

    tables = [table(np.cos, 2 * m, 1.0), table(np.sin, 2 * m, -1.0),
              table(np.cos, 2 * m + 1, 1.0), table(np.sin, 2 * m + 1, -1.0)]
    row4 = pl.BlockSpec((1, 2, rows, D_MODEL), lambda i: (i // nr, 0, i % nr, 0))
    tab_spec = pl.BlockSpec((rows, half_len), lambda i: (i % nr, 0))
    ab_spec = pl.BlockSpec((1, half_len, 2 * D_MODEL), lambda i: (i // nr, 0, 0))
    shape4 = (batch, 2, half_len, D_MODEL)
    out, h, route, cnt = pl.pallas_call(
        _fourier_split_router_kernel,
        grid=(batch * nr,),
        in_specs=[row4, _mod_spec(nr, row0)] + [tab_spec] * 4 + [ab_spec, ab_spec,
                  _const_spec((D_MODEL, D_MODEL)), _const_spec((1, D_MODEL)),
                  _const_spec((N_EXPERTS, D_MODEL))],
        out_specs=[row4, row4,
                   pl.BlockSpec((1, 2, 1, 8, rows), lambda i: (i // nr, 0, i % nr, 0, 0)),
                   pl.BlockSpec((1, 2, 1, N_EXPERTS, 128), lambda i: (i // nr, 0, i % nr, 0, 0))],
        out_shape=[jax.ShapeDtypeStruct(shape4, F32), jax.ShapeDtypeStruct(shape4, BF16),
                   jax.ShapeDtypeStruct((batch, 2, nr, 8, rows), F32),
                   jax.ShapeDtypeStruct((batch, 2, nr, N_EXPERTS, 128), F32)],
        compiler_params=_cparams(1),
        name="fourier_split_router",
    )(x2d.reshape(shape4), mod, *tables, a.reshape(batch, half_len, 2 * D_MODEL),
      b.reshape(batch, half_len, 2 * D_MODEL), wf, g.reshape(1, D_MODEL), wr_t)
    nb = t // rows
    return (out.reshape(t, D_MODEL), h.reshape(t, D_MODEL), route.reshape(nb, 8, rows),
            cnt.reshape(nb, N_EXPERTS, 128))


def _fourier_router(x2d, a, b, mod, wf, g, wr_t, batch, seq_len, row0):
    t = x2d.shape[0]
    rows = MOE_BLOCK
    if a.shape[1] == 2 * D_MODEL:
        return _fourier_split_router(x2d, a, b, mod, wf, g, wr_t, batch, seq_len, row0)
    part = min(seq_len, rows)
    nbat = rows // part
    nr = seq_len // part
    cl, sl = _dft_tables(seq_len)
    cl = jnp.asarray(cl, dtype=F32).astype(BF16)
    sl = jnp.asarray(-sl, dtype=F32).astype(BF16)
    row_spec = pl.BlockSpec((rows, D_MODEL), lambda i: (i, 0))
    tab_spec = pl.BlockSpec((part, seq_len), lambda i: (i % nr, 0))
    ab_spec = pl.BlockSpec((nbat, seq_len, D_MODEL), lambda i: (i // nr, 0, 0))
    nb = t // rows
    return pl.pallas_call(
        _fourier_router_kernel,
        grid=(nb,),
        in_specs=[row_spec, _mod_spec(nr, row0), tab_spec, tab_spec, ab_spec, ab_spec,
                  _const_spec((D_MODEL, D_MODEL)), _const_spec((1, D_MODEL)),
                  _const_spec((N_EXPERTS, D_MODEL))],
        out_specs=[row_spec, row_spec, pl.BlockSpec((1, 8, rows), lambda i: (i, 0, 0)),
                   pl.BlockSpec((1, N_EXPERTS, 128), lambda i: (i, 0, 0))],
        out_shape=[jax.ShapeDtypeStruct((t, D_MODEL), F32),
                   jax.ShapeDtypeStruct((t, D_MODEL), BF16),
                   jax.ShapeDtypeStruct((nb, 8, rows), F32),
                   jax.ShapeDtypeStruct((nb, N_EXPERTS, 128), F32)],
        compiler_params=_cparams(1),
        name="fourier_router",
    )(x2d, mod, cl, sl, a.reshape(batch, seq_len, D_MODEL), b.reshape(batch, seq_len, D_MODEL),
      wf, g.reshape(1, D_MODEL), wr_t)


def _segment_copies(n16, src_row, dst_row, make_copy):
    @pl.when(n16 > 0)
    def _():
        make_copy(pl.multiple_of(src_row, BF16_SUBLANES), pl.multiple_of(dst_row, BF16_SUBLANES),
                  pl.multiple_of(n16, BF16_SUBLANES))


def _dispatch_kernel(seg_ref, dst_ref, n16_ref, pad_dst_ref, pad_n_ref, nt_ref, *refs,
                     group_blocks, min_tiles):
    n_in = 2 * len(group_blocks)
    xs_ref, comp_ref, zero_ref, sem = refs[n_in:]
    i = pl.program_id(0)
    last = pl.num_programs(0) - 1
    slot = i % 2
    base = i * N_EXPERTS

    def block_copies(base_, slot_, act):
        for e in range(N_EXPERTS):
            _segment_copies(
                n16_ref[base_ + e], seg_ref[base_ + e], dst_ref[base_ + e],
                lambda s, d, n: act(pltpu.make_async_copy(
                    comp_ref.at[slot_, pl.ds(s, n)], xs_ref.at[pl.ds(d, n)], sem.at[slot_])))

    def zero_copies(act):
        for e in range(N_EXPERTS):
            _segment_copies(
                pad_n_ref[e], 0, pad_dst_ref[e],
                lambda s, d, n: act(pltpu.make_async_copy(
                    zero_ref.at[pl.ds(s, n)], xs_ref.at[pl.ds(d, n)], sem.at[2])))
        for tile in range(min_tiles, xs_ref.shape[0] // EXPERT_TILE):
            @pl.when(tile >= nt_ref[0])
            def _():
                act(pltpu.make_async_copy(
                    zero_ref, xs_ref.at[pl.ds(tile * EXPERT_TILE, EXPERT_TILE)], sem.at[2]))

    @pl.when(i == 0)
    def _():
        zero_ref[...] = jnp.zeros_like(zero_ref)
        zero_copies(lambda c: c.start())

    first = 0
    for grp, nb in enumerate(group_blocks):
        h_ref, route_ref = refs[2 * grp], refs[2 * grp + 1]

        @pl.when((i >= first) & (i < first + nb))
        def _():
            r = route_ref[0]
            pos1, pos2 = r[4:5], r[5:6]
            rows = lax.broadcasted_iota(jnp.int32, (MOE_ROWS, MOE_BLOCK), 0).astype(F32)
            onehot = jnp.where((rows == pos1) | (rows == pos2), 1.0, 0.0).astype(BF16)
            comp_ref[slot] = _dot(onehot, h_ref[...]).astype(BF16)
        first += nb
    block_copies(base, slot, lambda c: c.start())

    @pl.when(i > 0)
    def _():
        block_copies(base - N_EXPERTS, 1 - slot, lambda c: c.wait())

    @pl.when(i == last)
    def _():
        block_copies(base, slot, lambda c: c.wait())
        zero_copies(lambda c: c.wait())


def _dispatch(tables, hs, routes, n_rows, min_tiles):
    group_blocks = tuple(h.shape[0] // MOE_BLOCK for h in hs)
    in_specs, args, first = [], [], 0
    for h, route, nb in zip(hs, routes, group_blocks):
        blk = lambda i, first=first, nb=nb: jnp.clip(i - first, 0, nb - 1)
        in_specs += [pl.BlockSpec((MOE_BLOCK, D_MODEL), lambda i, *_, blk=blk: (blk(i), 0)),
                     pl.BlockSpec((1, 8, MOE_BLOCK), lambda i, *_, blk=blk: (blk(i), 0, 0))]
        args += [h, route]
        first += nb
    grid_spec = pltpu.PrefetchScalarGridSpec(
        num_scalar_prefetch=len(tables),
        grid=(sum(group_blocks),),
        in_specs=in_specs,
        out_specs=pl.BlockSpec(memory_space=pl.ANY),
        scratch_shapes=[pltpu.VMEM((2, MOE_ROWS, D_MODEL), BF16),
                        pltpu.VMEM((EXPERT_TILE, D_MODEL), BF16),
                        pltpu.SemaphoreType.DMA((3,))],
    )
    return pl.pallas_call(
        functools.partial(_dispatch_kernel, group_blocks=group_blocks, min_tiles=min_tiles),
        grid_spec=grid_spec,
        out_shape=jax.ShapeDtypeStruct((n_rows, D_MODEL), BF16),
        compiler_params=_cparams(1),
        name="moe_dispatch",
    )(*tables, *args)


def _expert_kernel(te_ref, nt_ref, catch_ref, bge_ref, bgc_ref, xs_ref, wgu_hbm, wd_hbm, ys_ref,
                   wgu_bf, wd_bf, stage_gu, stage_d, sem):
    i = pl.program_id(0)
    used = i < nt_ref[0]
    rg, rd = D_MODEL // W_CHUNKS, D_FF // W_CHUNKS

    def chunk_copies(expert, c, slot):
        return (pltpu.make_async_copy(wgu_hbm.at[expert, pl.ds(pl.multiple_of(c * rg, rg), rg)],
                                      stage_gu.at[slot], sem.at[0, slot]),
                pltpu.make_async_copy(wd_hbm.at[expert, pl.ds(pl.multiple_of(c * rd, rd), rd)],
                                      stage_d.at[slot], sem.at[1, slot]))

    def cast_chunk(expert, c, slot):
        ws = expert % 2
        wgu_bf[ws, pl.ds(pl.multiple_of(c * rg, rg), rg), :] = stage_gu[slot].astype(BF16)
        wd_bf[ws, pl.ds(pl.multiple_of(c * rd, rd), rd), :] = stage_d[slot].astype(BF16)

    @pl.when(used)
    def _():
        e = te_ref[i]

        @pl.when(i == 0)
        def _():
            for cp in chunk_copies(bge_ref[0], bgc_ref[0], 0):
                cp.start()

        @pl.when(i + 1 < nt_ref[0])
        def _():
            for cp in chunk_copies(bge_ref[i + 1], bgc_ref[i + 1], (i + 1) % 2):
                cp.start()

        def catch_up(c, carry):
            cps = chunk_copies(e, c, 2)
            for cp in cps:
                cp.start()
            for cp in cps:
                cp.wait()
            cast_chunk(e, c, 2)
            return carry

        lax.fori_loop(catch_ref[i], W_CHUNKS, catch_up, 0)

        for cp in chunk_copies(bge_ref[i], bgc_ref[i], i % 2):
            cp.wait()
        cast_chunk(bge_ref[i], bgc_ref[i], i % 2)
        ws = e % 2
        ys_ref[...] = _swiglu(xs_ref[...], wgu_bf.at[ws], wd_bf.at[ws]).astype(ys_ref.dtype)

    @pl.when(jnp.logical_not(used))
    def _():
        ys_ref[...] = jnp.zeros_like(ys_ref)


def _experts(tile_tables, xs, wgu_e, wd_e):
    rows = xs.shape[0]
    tm = EXPERT_TILE
    rg, rd = D_MODEL // W_CHUNKS, D_FF // W_CHUNKS
    grid_spec = pltpu.PrefetchScalarGridSpec(
        num_scalar_prefetch=len(tile_tables),
        grid=(rows // tm,),
        in_specs=[pl.BlockSpec((tm, D_MODEL), lambda i, te, nt, *_: (jnp.minimum(i, nt[0] - 1), 0)),
                  pl.BlockSpec(memory_space=pl.ANY), pl.BlockSpec(memory_space=pl.ANY)],
        out_specs=pl.BlockSpec((tm, D_MODEL), lambda i, *_: (i, 0)),
        scratch_shapes=[pltpu.VMEM((2, D_MODEL, 2 * D_FF), BF16), pltpu.VMEM((2, D_FF, D_MODEL), BF16),
                        pltpu.VMEM((3, rg, 2 * D_FF), F32), pltpu.VMEM((3, rd, D_MODEL), F32),
                        pltpu.SemaphoreType.DMA((2, 3))],
    )
    return pl.pallas_call(
        _expert_kernel,
        grid_spec=grid_spec,
        out_shape=jax.ShapeDtypeStruct((rows, D_MODEL), BF16),
        compiler_params=_cparams(1),
        name="moe_experts",
    )(*tile_tables, xs, wgu_e, wd_e)


def _combine_kernel(seg_ref, dst_ref, n16_ref, x_ref, rt_ref, mod_ref, g_ref, ys_ref, out_ref,
                    buf_ref, sem, *, block0):
    i = pl.program_id(0)
    slot = i % 2
    base = (block0 + i) * N_EXPERTS

    def block_copies(base_, slot_, act):
        for e in range(N_EXPERTS):
            _segment_copies(
                n16_ref[base_ + e], seg_ref[base_ + e], dst_ref[base_ + e],
                lambda s, d, n: act(pltpu.make_async_copy(
                    ys_ref.at[pl.ds(d, n)], buf_ref.at[slot_, pl.ds(s, n)], sem.at[slot_])))

    def fetch(base_, slot_):
        buf_ref[slot_] = jnp.zeros(buf_ref.shape[1:], buf_ref.dtype)
        block_copies(base_, slot_, lambda c: c.start())

    @pl.when(i == 0)
    def _():
        fetch(base, slot)

    @pl.when(i + 1 < pl.num_programs(0))
    def _():
        fetch(base + N_EXPERTS, 1 - slot)

    block_copies(base, slot, lambda c: c.wait())
    rt = rt_ref[...]
    pos1, pos2 = rt[:, 4:5], rt[:, 5:6]
    cols = lax.broadcasted_iota(jnp.int32, (MOE_BLOCK, MOE_ROWS), 1).astype(F32)
    gates = (jnp.where(cols == pos1, rt[:, 2:3], 0.0)
             + jnp.where(cols == pos2, rt[:, 3:4], 0.0)).astype(BF16)
    x = x_ref[...] + mod_ref[0, 5:6, :] * _dot(gates, buf_ref[slot])
    out_ref[...] = _rms(x, g_ref[...], EPS)


def _combine(tables, x2d, route_t, mod, g, ys, seq_len, row0, block0):
    t = x2d.shape[0]
    tb = MOE_BLOCK
    grid_spec = pltpu.PrefetchScalarGridSpec(
        num_scalar_prefetch=len(tables),
        grid=(t // tb,),
        in_specs=[pl.BlockSpec((tb, D_MODEL), lambda i, *_: (i, 0)),
                  pl.BlockSpec((tb, 8), lambda i, *_: (i, 0)),
                  _mod_spec(max(seq_len // tb, 1), row0),
                  pl.BlockSpec((1, D_MODEL), lambda i, *_: (0, 0)),
                  pl.BlockSpec(memory_space=pl.ANY)],
        out_specs=pl.BlockSpec((tb, D_MODEL), lambda i, *_: (i, 0)),
        scratch_shapes=[pltpu.VMEM((2, MOE_ROWS, D_MODEL), BF16), pltpu.SemaphoreType.DMA((2,))],
    )
    return pl.pallas_call(
        functools.partial(_combine_kernel, block0=block0),
        grid_spec=grid_spec,
        out_shape=jax.ShapeDtypeStruct((t, D_MODEL), F32),
        compiler_params=_cparams(1),
        name="moe_combine",
    )(*tables, x2d, route_t, mod, g.reshape(1, D_MODEL), ys)


def _moe_tables(counts, n_rows):
    pad = BF16_SUBLANES
    n16 = (counts + pad - 1) // pad * pad
    seg = jnp.cumsum(n16, axis=1) - n16
    total = jnp.sum(n16, axis=0)
    region = (total + EXPERT_TILE - 1) // EXPERT_TILE * EXPERT_TILE
    region_end = jnp.cumsum(region)
    region_start = region_end - region
    dst = region_start[None, :] + jnp.cumsum(n16, axis=0) - n16
    tiles_end = region_end // EXPERT_TILE
    tile_ids = jnp.arange(n_rows // EXPERT_TILE, dtype=jnp.int32)
    tile_expert = jnp.minimum(jnp.sum(tile_ids[:, None] >= tiles_end[None, :], axis=1), N_EXPERTS - 1)
    n_tiles_e = region // EXPERT_TILE
    local = tile_ids - (tiles_end - n_tiles_e)[tile_expert]
    prev_tiles = jnp.concatenate([jnp.zeros((1,), n_tiles_e.dtype), n_tiles_e[:-1]])
    catch_from = jnp.where(local == 0, jnp.minimum(prev_tiles[tile_expert], W_CHUNKS), W_CHUNKS)
    ahead_expert = jnp.where(tile_expert < N_EXPERTS - 1, tile_expert + 1, N_EXPERTS - 2)
    ahead_chunk = jnp.clip(local, 0, W_CHUNKS - 1)
    flat = lambda a: a.reshape(-1).astype(jnp.int32)
    block_tables = (flat(seg), flat(dst), flat(n16))
    pad_tables = (flat(region_start + total), flat(region - total))
    tile_tables = (flat(tile_expert), flat(tiles_end[-1:]), flat(catch_from), flat(ahead_expert),
                   flat(ahead_chunk))
    return block_tables, pad_tables, tile_tables


def kernel(x_prompt, x_sample, c, cache_k_0, cache_v_0, c_ctx, ada_w_0, ada_b_0, norm1_g_0, norm2_g_0, w_qkv_0, lambda_q1_0, lambda_k1_0, lambda_q2_0, lambda_k2_0, subln_g_0, w_o_0, w_gu_0, w_down_0, ada_w_1, ada_b_1, norm1_g_1, norm2_g_1, w_fourier_1, w_router_1, w_gu_e_1, w_down_e_1, final_norm_g):
    bp, lp, _ = x_prompt.shape
    bs, ls, _ = x_sample.shape
    assert 1 + bs <= ADA_ROWS and (bp * lp) % MOE_BLOCK == 0 and ls % MOE_BLOCK == 0
    assert MOE_BLOCK % lp == 0 or lp % MOE_BLOCK == 0

    cond = jnp.zeros((ADA_ROWS, D_MODEL), F32).at[0].set(c_ctx).at[1:1 + bs].set(c)
    mod0 = _adaln(cond, ada_w_0, ada_b_0)
    mod1 = _adaln(cond, ada_w_1, ada_b_1)
    lam_params = jnp.stack([lambda_q1_0, lambda_k1_0, lambda_q2_0, lambda_k2_0])

    w_qkv = w_qkv_0.astype(BF16)
    w_o = w_o_0.astype(BF16)
    w_gu = w_gu_0.astype(BF16)
    w_down = w_down_0.astype(BF16)
    w_f = w_fourier_1.astype(BF16)
    w_router_t = w_router_1.T

    groups = [dict(x=x_prompt.reshape(bp * lp, D_MODEL), batch=bp, seq=lp, row0=0, rope=False),
              dict(x=x_sample.reshape(bs * ls, D_MODEL), batch=bs, seq=ls, row0=1, rope=True)]

    k_ctx = v_ctx = None
    for gr in groups:
        x, batch, seq, row0 = gr["x"], gr["batch"], gr["seq"], gr["row0"]
        q, k, v = _qkv(x, mod0, norm1_g_0, w_qkv, seq, row0, gr["rope"], BF16 if gr["rope"] else F32)
        if gr["rope"]:
            o = _attention(lam_params, subln_g_0, q, k, v, batch, seq, cache_k_0, cache_v_0,
                           tq=SAMPLE_Q_TILE, heads=SAMPLE_HEADS, sub=SAMPLE_Q_SUB, lag=ATTN_LAG)
        else:
            k_ctx, v_ctx = k, v
            o = _attention(lam_params, subln_g_0, q, k, v, batch, seq, tq=seq, heads=N_HEADS, sub=seq,
                           lag=ATTN_LAG)
        x, fa, fb = _post_attn(x, o, mod0, mod1, norm2_g_0, norm1_g_1, w_o, w_gu, w_down, seq, row0,
                               pair_rows=seq % (2 * MOE_BLOCK) == 0)
        gr["x"], gr["h"], gr["route"], gr["cnt"] = _fourier_router(
            x, fa, fb, mod1, w_f, norm2_g_1, w_router_t, batch, seq, row0)

    n_blocks = [gr["x"].shape[0] // MOE_BLOCK for gr in groups]
    n_pairs = 2 * sum(gr["x"].shape[0] for gr in groups)
    max_rows = n_pairs + sum(n_blocks) * N_EXPERTS * (BF16_SUBLANES - 1) + N_EXPERTS * EXPERT_TILE
    max_rows = (max_rows + EXPERT_TILE - 1) // EXPERT_TILE * EXPERT_TILE
    counts = jnp.concatenate([gr["cnt"][:, :, 0] for gr in groups], axis=0).astype(jnp.int32)
    block_tables, pad_tables, tile_tables = _moe_tables(counts, max_rows)

    xs = _dispatch(block_tables + pad_tables + tile_tables[1:2], [gr["h"] for gr in groups],
                   [gr["route"] for gr in groups], max_rows, n_pairs // EXPERT_TILE)
    ys = _experts(tile_tables, xs, w_gu_e_1, w_down_e_1)
    outs = []
    block0 = 0
    for gr, nb in zip(groups, n_blocks):
        route_t = gr["route"].transpose(0, 2, 1).reshape(-1, 8)
        outs.append(_combine(block_tables, gr["x"], route_t, mod1, final_norm_g, ys,
                             gr["seq"], gr["row0"], block0))
        block0 += nb

    y_prompt = outs[0].reshape(bp, lp, D_MODEL)
    y_sample = outs[1].reshape(bs, ls, D_MODEL)
    return (y_prompt, y_sample,
            k_ctx.reshape(bp, lp, N_HEADS, 2 * HEAD_DIM), v_ctx.reshape(bp, lp, N_HEADS, V_DIM))
```

```python
import functools
import math

import jax
import jax.numpy as jnp
import numpy as np
from jax import lax
from jax.experimental import pallas as pl
from jax.experimental.pallas import tpu as pltpu

F32 = jnp.float32
BF16 = jnp.bfloat16

D_MODEL = 1024
N_HEADS = 8
HEAD_DIM = 64
V_DIM = 2 * HEAD_DIM
GRID_W = 64
AXIS_DIM = HEAD_DIM // 2
ROPE_THETA = 10000.0
N_FOURIER_GROUPS = 4
FOURIER_GROUP = D_MODEL // N_FOURIER_GROUPS
D_FF = 2816
N_EXPERTS = 8
N_MOD = 6
EPS = 1e-6
SUBLN_EPS = 1e-5
LAMBDA_INIT_0 = 0.8 - 0.6 * math.exp(-0.3 * 0)
Q_SCALE = HEAD_DIM ** -0.5 * math.log2(math.e)

V7X_VMEM_BYTES = 64 * 1024 * 1024
VMEM_LIMIT = V7X_VMEM_BYTES - 8 * 1024 * 1024
V7X_MXU_DIM = 256
BF16_SUBLANES = 16

ADA_ROWS = 16
ADA_TN = 1536
QKV_TILE = 1024
FFN_TILE = 512
FF_SPLITS = (0, 6 * V7X_MXU_DIM, D_FF)
SAMPLE_Q_TILE = 2048
SAMPLE_HEADS = 1
SAMPLE_Q_SUB = 128
ATTN_LAG = 1
MOE_BLOCK = 512
MOE_ROWS = 2 * MOE_BLOCK + 128
EXPERT_TILE = 512
W_CHUNKS = 8

_NT = (((1,), (1,)), ((), ()))


def _dot(a, b):
    return jnp.dot(a, b, preferred_element_type=F32)


def _split_bf16(x):
    hi = x.astype(BF16)
    lo = (x - hi.astype(F32)).astype(BF16)
    return hi, lo


def _rms(x, g, eps):
    return x * lax.rsqrt(jnp.mean(x * x, axis=-1, keepdims=True) + eps) * g


def _cparams(n_grid, vmem=VMEM_LIMIT):
    return pltpu.CompilerParams(dimension_semantics=("arbitrary",) * n_grid, vmem_limit_bytes=vmem)


def _const_spec(shape):
    nd = len(shape)
    return pl.BlockSpec(shape, lambda *_: (0,) * nd, pipeline_mode=pl.Buffered(1))


def _adaln_kernel(c_ref, w0_ref, w1_ref, b0_ref, b1_ref, o_ref):
    first = pl.program_id(0) < pl.num_programs(0) // 2
    c = c_ref[...]
    a_hi, a_lo = _split_bf16(c * jax.nn.sigmoid(c))
    w_hi, w_lo = _split_bf16(jnp.where(first, w0_ref[...], w1_ref[...]))
    both = _dot(jnp.concatenate([a_hi, a_lo], axis=0), w_hi)
    bias = jnp.where(first, b0_ref[...], b1_ref[...])
    o_ref[0] = both[:ADA_ROWS] + both[ADA_ROWS:] + _dot(a_hi, w_lo) + bias


def _adaln(cond, w0, b0, w1, b1):
    n = N_MOD * D_MODEL
    nt = n // ADA_TN
    lo = lambda j: (0, jnp.minimum(j, nt - 1))
    hi = lambda j: (0, jnp.maximum(j - nt, 0))
    out = pl.pallas_call(
        _adaln_kernel,
        grid=(2 * nt,),
        in_specs=[pl.BlockSpec((ADA_ROWS, D_MODEL), lambda j: (0, 0)),
                  pl.BlockSpec((D_MODEL, ADA_TN), lo), pl.BlockSpec((D_MODEL, ADA_TN), hi),
                  pl.BlockSpec((1, ADA_TN), lo), pl.BlockSpec((1, ADA_TN), hi)],
        out_specs=pl.BlockSpec((1, ADA_ROWS, ADA_TN), lambda j: (j // nt, 0, j % nt)),
        out_shape=jax.ShapeDtypeStruct((2, ADA_ROWS, n), F32),
        compiler_params=_cparams(1),
        name="adaln",
    )(cond, w0, w1, b0.reshape(1, n), b1.reshape(1, n))
    out = out.reshape(2, ADA_ROWS, N_MOD, D_MODEL)
    return out[0], out[1]


def _mod_spec(seq_tiles, row0):
    if row0 == 0:
        return pl.BlockSpec((1, N_MOD, D_MODEL), lambda i, *_: (0, 0, 0))
    return pl.BlockSpec((1, N_MOD, D_MODEL), lambda i, *_: (row0 + i // seq_tiles, 0, 0))


def _qkv_kernel(*refs, rope):
    if rope:
        x_ref, mod_ref, g_ref, w_ref, cos_ref, sa_ref, sb_ref, q_ref, k_ref, v_ref = refs
    else:
        x_ref, mod_ref, g_ref, w_ref, q_ref, k_ref, v_ref = refs
    h = _rms(x_ref[...], g_ref[...], EPS)
    h = h * (1.0 + mod_ref[0, 1:2, :]) + mod_ref[0, 0:1, :]
    qkv = _dot(h.astype(BF16), w_ref[...])
    inner = N_HEADS * 2 * HEAD_DIM
    for which, ref in ((0, q_ref), (1, k_ref), (2, v_ref)):
        part = qkv[:, which * inner:(which + 1) * inner]
        if rope and which < 2:
            cos, sa, sb = cos_ref[...], sa_ref[...], sb_ref[...]
            for hd in range(N_HEADS):
                blk = part[:, hd * V_DIM:(hd + 1) * V_DIM]
                blk = (blk * cos + pltpu.roll(blk, V_DIM - AXIS_DIM // 2, 1) * sa
                       + pltpu.roll(blk, AXIS_DIM // 2, 1) * sb)
                if which == 0:
                    blk = blk * Q_SCALE
                ref[:, hd * V_DIM:(hd + 1) * V_DIM] = blk.astype(ref.dtype)
        else:
            if which == 0:
                part = part * Q_SCALE
            ref[...] = part.astype(ref.dtype)


def _rope_tables(length):
    rows = length // GRID_W
    row = jnp.repeat(jnp.arange(rows), GRID_W).astype(F32)
    col = jnp.tile(jnp.arange(GRID_W), rows).astype(F32)
    inv = 1.0 / (ROPE_THETA ** (jnp.arange(0, AXIS_DIM, 2, dtype=F32) / AXIS_DIM))
    ar = row[:, None] * inv[None, :]
    ac = col[:, None] * inv[None, :]
    ang = jnp.concatenate([ar, ar, ac, ac], axis=-1)
    cos, sin = jnp.cos(ang), jnp.sin(ang)
    first = (jnp.arange(HEAD_DIM) % AXIS_DIM) < (AXIS_DIM // 2)
    sin_a = jnp.where(first[None, :], -sin, 0.0)
    sin_b = jnp.where(first[None, :], 0.0, sin)
    wide = lambda t: jnp.concatenate([t, t], axis=-1)
    return wide(cos), wide(sin_a), wide(sin_b)


def _qkv(x2d, mod, g, w_bf16, seq_len, row0, rope, kv_dtype):
    t = x2d.shape[0]
    tm = QKV_TILE
    seq_tiles = seq_len // tm
    row_spec = pl.BlockSpec((tm, D_MODEL), lambda i: (i, 0))
    in_specs = [row_spec, _mod_spec(seq_tiles, row0), _const_spec((1, D_MODEL)),
                _const_spec((D_MODEL, 3 * D_MODEL))]
    args = [x2d, mod, g.reshape(1, D_MODEL), w_bf16]
    if rope:
        tab_spec = pl.BlockSpec((tm, V_DIM), lambda i: (i % seq_tiles, 0))
        in_specs += [tab_spec] * 3
        args += list(_rope_tables(seq_len))
    return pl.pallas_call(
        functools.partial(_qkv_kernel, rope=rope),
        grid=(t // tm,),
        in_specs=in_specs,
        out_specs=[row_spec] * 3,
        out_shape=[jax.ShapeDtypeStruct((t, D_MODEL), BF16),
                   jax.ShapeDtypeStruct((t, D_MODEL), kv_dtype),
                   jax.ShapeDtypeStruct((t, D_MODEL), kv_dtype)],
        compiler_params=_cparams(1),
        name="qkv",
    )(*args)


def _attn_kernel(*refs, has_cache, heads, sub, lag):
    if has_cache:
        lam_ref, sg_ref, q_ref, k_ref, v_ref, ck_ref, cv_ref, o_ref = refs
    else:
        lam_ref, sg_ref, q_ref, k_ref, v_ref, o_ref = refs
    lp = lam_ref[...]
    lam = (jnp.exp(jnp.sum(lp[0:1] * lp[1:2], axis=-1, keepdims=True))
           - jnp.exp(jnp.sum(lp[2:3] * lp[3:4], axis=-1, keepdims=True)) + LAMBDA_INIT_0)
    lane = lax.broadcasted_iota(jnp.int32, (1, V_DIM), 1)
    sg = sg_ref[...] * (1.0 - LAMBDA_INIT_0)
    def head_keys(hd):
        cols = slice(hd * V_DIM, (hd + 1) * V_DIM)
        k, v = k_ref[0, :, cols].astype(BF16), v_ref[0, :, cols].astype(BF16)
        if has_cache:
            k = jnp.concatenate([k, ck_ref[0, :, cols].astype(BF16)], axis=0)
            v = jnp.concatenate([v, cv_ref[0, :, cols].astype(BF16)], axis=0)
        return [(k, v)]

    chains = [(hd, r0, comp) for hd in range(heads) for r0 in range(0, q_ref.shape[0], sub)
              for comp in range(2)]
    keys = {hd: head_keys(hd) for hd in range(heads)}
    scores, probs, outs = {}, {}, {}

    def stage_scores(c):
        hd, r0, comp = chains[c]
        q = q_ref[r0:r0 + sub, hd * V_DIM:(hd + 1) * V_DIM]
        sel = (lane < HEAD_DIM) if comp == 0 else (lane >= HEAD_DIM)
        qc = jnp.where(sel, q, jnp.zeros_like(q))
        scores[c] = [lax.dot_general(qc, k, _NT, preferred_element_type=F32) for k, _ in keys[hd]]

    def stage_softmax(c):
        s = scores.pop(c)
        m = functools.reduce(jnp.maximum, [jnp.max(x, axis=-1, keepdims=True) for x in s])
        p = [jnp.exp2(x - m) for x in s]
        l = functools.reduce(jnp.add, [jnp.sum(x, axis=-1, keepdims=True) for x in p])
        probs[c] = ([x.astype(BF16) for x in p], l)

    def stage_values(c):
        hd, r0, comp = chains[c]
        p, l = probs.pop(c)
        o = functools.reduce(jnp.add, [_dot(x, v) for x, (_, v) in zip(p, keys[hd])])
        outs[c] = o * (1.0 / l)
        if comp == 1:
            o = outs.pop(c - 1) - lam * outs.pop(c)
            o_ref[r0:r0 + sub, hd * V_DIM:(hd + 1) * V_DIM] = _rms(o, sg, SUBLN_EPS).astype(o_ref.dtype)

    for t in range(len(chains) + 2 * lag):
        if t < len(chains):
            stage_scores(t)
        if 0 <= t - lag < len(chains):
            stage_softmax(t - lag)
        if 0 <= t - 2 * lag < len(chains):
            stage_values(t - 2 * lag)


def _attention(lam_params, subln_g, q, k, v, batch, seq_len, cache_k=None, cache_v=None, *,
               tq, heads, sub, lag):
    nq = seq_len // tq
    has_cache = cache_k is not None
    width = heads * V_DIM
    k3 = k.reshape(batch, seq_len, D_MODEL)
    v3 = v.reshape(batch, seq_len, D_MODEL)
    q_spec = pl.BlockSpec((tq, width), lambda b, h, i: (b * nq + i, h))
    kv_spec = pl.BlockSpec((1, seq_len, width), lambda b, h, i: (b, 0, h))
    in_specs = [_const_spec((4, HEAD_DIM)), _const_spec((1, V_DIM)), q_spec, kv_spec, kv_spec]
    args = [lam_params, subln_g.reshape(1, V_DIM), q, k3, v3]
    if has_cache:
        past = cache_k.shape[1]
        c_spec = pl.BlockSpec((1, past, width), lambda b, h, i: (b, 0, h))
        in_specs += [c_spec, c_spec]
        args += [cache_k.reshape(batch, past, D_MODEL), cache_v.reshape(batch, past, D_MODEL)]
    return pl.pallas_call(
        functools.partial(_attn_kernel, has_cache=has_cache, heads=heads, sub=sub, lag=lag),
        grid=(batch, N_HEADS // heads, nq),
        in_specs=in_specs,
        out_specs=q_spec,
        out_shape=jax.ShapeDtypeStruct((batch * seq_len, D_MODEL), BF16),
        compiler_params=_cparams(3),
        name="diff_attn",
    )(*args)


def _swiglu(h, wgu, wd):
    out = None
    for c0, c1 in zip(FF_SPLITS[:-1], FF_SPLITS[1:]):
        g = _dot(h, wgu[:, c0:c1])
        u = _dot(h, wgu[:, D_FF + c0:D_FF + c1])
        down = _dot((g * jax.nn.sigmoid(g) * u).astype(BF16), wd[c0:c1, :])
        out = down if out is None else out + down
    return out


def _dft_tables(n):
    j = np.arange(n, dtype=np.int64)
    ang = (2.0 * np.pi / n) * ((j[:, None] * j[None, :]) % n).astype(np.float64)
    s = n ** -0.5
    return np.cos(ang) * s, np.sin(ang) * s


def _post_attn_kernel(x_ref, o_ref, mod0_ref, mod1_ref, g2_ref, g1n_ref, wo_ref, wgu_ref, wd_ref,
                      cs_ref, out_ref, a_ref, b_ref, *, pair_rows):
    x1 = x_ref[...] + mod0_ref[0, 2:3, :] * _dot(o_ref[...], wo_ref[...])
    h = _rms(x1, g2_ref[...], EPS) * (1.0 + mod0_ref[0, 4:5, :]) + mod0_ref[0, 3:4, :]
    x2 = x1 + mod0_ref[0, 5:6, :] * _swiglu(h.astype(BF16), wgu_ref, wd_ref)
    out_ref[...] = x2
    h1 = _rms(x2, g1n_ref[...], EPS) * (1.0 + mod1_ref[0, 1:2, :]) + mod1_ref[0, 0:1, :]
    hb = h1.astype(BF16)
    fg = FOURIER_GROUP
    n = hb.shape[0]
    if pair_rows:
        row = lax.broadcasted_iota(jnp.int32, (n, n), 0)
        tok = lax.broadcasted_iota(jnp.int32, (n, n), 1)
        src = jnp.where(row < n // 2, 2 * row, 2 * row - (n - 1))
        hb = _dot(jnp.where(tok == src, 1.0, 0.0).astype(BF16), hb).astype(BF16)
    for grp in range(N_FOURIER_GROUPS):
        ab = _dot(hb[:, grp * fg:(grp + 1) * fg], cs_ref[...])
        cols = slice(grp * fg, (grp + 1) * fg)
        if pair_rows:
            odd_cols = slice(D_MODEL + grp * fg, D_MODEL + (grp + 1) * fg)
            a_ref[:, cols] = ab[:n // 2, :fg].astype(BF16)
            a_ref[:, odd_cols] = ab[n // 2:, :fg].astype(BF16)
            b_ref[:, cols] = ab[:n // 2, fg:].astype(BF16)
            b_ref[:, odd_cols] = ab[n // 2:, fg:].astype(BF16)
        else:
            a_ref[:, cols] = ab[:, :fg].astype(BF16)
            b_ref[:, cols] = ab[:, fg:].astype(BF16)


def _post_attn(x2d, o, mod0, mod1, g2, g1n, wo, wgu, wd, seq_len, row0, pair_rows):
    t = x2d.shape[0]
    tm = FFN_TILE if row0 == 0 else min(FFN_TILE, seq_len)
    cd, sd = _dft_tables(FOURIER_GROUP)
    cs = jnp.asarray(np.concatenate([cd, sd], axis=1), dtype=F32).astype(BF16)
    row_spec = pl.BlockSpec((tm, D_MODEL), lambda i: (i, 0))
    mod_spec = _mod_spec(seq_len // tm, row0)
    if pair_rows:
        ab_spec = pl.BlockSpec((tm // 2, 2 * D_MODEL), lambda i: (i, 0))
        ab_shape = jax.ShapeDtypeStruct((t // 2, 2 * D_MODEL), BF16)
    else:
        ab_spec, ab_shape = row_spec, jax.ShapeDtypeStruct((t, D_MODEL), BF16)
    return pl.pallas_call(
        functools.partial(_post_attn_kernel, pair_rows=pair_rows),
        grid=(t // tm,),
        in_specs=[row_spec, row_spec, mod_spec, mod_spec, _const_spec((1, D_MODEL)),
                  _const_spec((1, D_MODEL)), _const_spec((D_MODEL, D_MODEL)),
                  _const_spec((D_MODEL, 2 * D_FF)), _const_spec((D_FF, D_MODEL)),
                  _const_spec((FOURIER_GROUP, 2 * FOURIER_GROUP))],
        out_specs=[row_spec, ab_spec, ab_spec],
        out_shape=[jax.ShapeDtypeStruct((t, D_MODEL), F32), ab_shape, ab_shape],
        compiler_params=_cparams(1),
        name="post_attn_swiglu",
    )(x2d, o, mod0, mod1, g2.reshape(1, D_MODEL), g1n.reshape(1, D_MODEL), wo, wgu, wd, cs)


def _route(x, mod_ref, g_ref, wr_ref, h_ref, route_ref, cnt_ref):
    tb = MOE_BLOCK
    h = _rms(x, g_ref[...], EPS) * (1.0 + mod_ref[0, 4:5, :]) + mod_ref[0, 3:4, :]
    h_hi, h_lo = _split_bf16(h)
    h_ref[...] = h_hi
    w_hi, w_lo = _split_bf16(wr_ref[...])
    dg = lambda a, b: lax.dot_general(a, b, _NT, preferred_element_type=F32)
    logits = dg(w_hi, h_hi) + dg(w_lo, h_hi) + dg(w_hi, h_lo)
    e = jnp.exp(logits - jnp.max(logits, axis=0, keepdims=True))
    probs = e / jnp.sum(e, axis=0, keepdims=True)
    eidx = lax.broadcasted_iota(jnp.int32, (N_EXPERTS, tb), 0).astype(F32)
    big = float(N_EXPERTS)
    p1 = jnp.max(probs, axis=0, keepdims=True)
    i1 = jnp.min(jnp.where(probs == p1, eidx, big), axis=0, keepdims=True)
    oh1 = eidx == i1
    rest = jnp.where(oh1, -1.0, probs)
    p2 = jnp.max(rest, axis=0, keepdims=True)
    i2 = jnp.min(jnp.where(rest == p2, eidx, big), axis=0, keepdims=True)
    oh2 = eidx == i2
    den = p1 + p2
    oh = jnp.where(oh1 | oh2, 1.0, 0.0)
    before = (lax.broadcasted_iota(jnp.int32, (tb, tb), 0)
              < lax.broadcasted_iota(jnp.int32, (tb, tb), 1))
    rank = _dot(oh.astype(BF16), jnp.where(before, 1.0, 0.0).astype(BF16))
    cnt = jnp.sum(oh, axis=1, keepdims=True)
    cnt_ref[0] = jnp.broadcast_to(cnt, (N_EXPERTS, 128))
    n16 = jnp.floor((cnt + (BF16_SUBLANES - 1.0)) * (1.0 / BF16_SUBLANES)) * BF16_SUBLANES
    ecol = lax.broadcasted_iota(jnp.int32, (N_EXPERTS, 1), 0)
    seg = jnp.zeros_like(n16)
    for ex in range(N_EXPERTS - 1):
        seg = seg + jnp.where(ecol > ex, n16[ex:ex + 1, :], 0.0)
    pos = rank + seg
    r1 = jnp.sum(jnp.where(oh1, pos, 0.0), axis=0, keepdims=True)
    r2 = jnp.sum(jnp.where(oh2, pos, 0.0), axis=0, keepdims=True)
    zero = jnp.zeros_like(r1)
    route_ref[...] = jnp.concatenate([i1, i2, p1 / den, p2 / den, r1, r2, zero, zero], axis=0)


def _fourier_router_kernel(x_ref, mod_ref, cl_ref, sl_ref, a_ref, b_ref, wf_ref, g_ref, wr_ref,
                           out_ref, h_ref, route_ref, cnt_ref):
    ys = [_dot(cl_ref[...], a_ref[j]) + _dot(sl_ref[...], b_ref[j]) for j in range(a_ref.shape[0])]
    y = ys[0] if len(ys) == 1 else jnp.concatenate(ys, axis=0)
    x = x_ref[...] + mod_ref[0, 2:3, :] * _dot(y.astype(BF16), wf_ref[...])
    out_ref[...] = x
    _route(x, mod_ref, g_ref, wr_ref, h_ref, route_ref.at[0], cnt_ref)


def _fourier_split_router_kernel(x_ref, mod_ref, ce_ref, se_ref, co_ref, so_ref, a_ref, b_ref, wf_ref,
                                 g_ref, wr_ref, out_ref, h_ref, route_ref, cnt_ref):
    half_len = a_ref.shape[2] // 2
    even = _dot(ce_ref[...], a_ref[0, :, :half_len]) + _dot(se_ref[...], b_ref[0, :, :half_len])
    odd = _dot(co_ref[...], a_ref[0, :, half_len:]) + _dot(so_ref[...], b_ref[0, :, half_len:])
    xs = []
    for half, y in ((0, even + odd), (1, even - odd)):
        x = x_ref[0, half] + mod_ref[0, 2:3, :] * _dot(y.astype(BF16), wf_ref[...])
        out_ref[0, half] = x
        xs.append(x)
    for half, x in enumerate(xs):
        _route(x, mod_ref, g_ref, wr_ref, h_ref.at[0, half], route_ref.at[0, half, 0],
               cnt_ref.at[0, half])


def _fourier_split_router(x2d, a, b, mod, wf, g, wr_t, batch, seq_len, row0):
    t = x2d.shape[0]
    rows = MOE_BLOCK
    half_len = seq_len // 2
    nr = half_len // rows
    j = np.arange(half_len, dtype=np.int64)[:, None]
    m = np.arange(half_len, dtype=np.int64)[None, :]
    scale = seq_len ** -0.5

    def table(fn, k, sign):
        ang = (2.0 * np.pi / seq_len) * ((j * k) % seq_len).astype(np.float64)
        return jnp.asarray(sign * scale * fn(ang), dtype=F32).astype(BF16)

    tables = [table(np.cos, 2 * m, 1.0), table(np.sin, 2 * m, -1.0),
              table(np.cos, 2 * m + 1, 1.0), table(np.sin, 2 * m + 1, -1.0)]
    row4 = pl.BlockSpec((1, 2, rows, D_MODEL), lambda i: (i // nr, 0, i % nr, 0))
    tab_spec = pl.BlockSpec((rows, half_len), lambda i: (i % nr, 0))
    ab_spec = pl.BlockSpec((1, half_len, 2 * D_MODEL), lambda i: (i // nr, 0, 0))
    shape4 = (batch, 2, half_len, D_MODEL)
    out, h, route, cnt = pl.pallas_call(
        _fourier_split_router_kernel,
        grid=(batch * nr,),
        in_specs=[row4, _mod_spec(nr, row0)] + [tab_spec] * 4 + [ab_spec, ab_spec,
                  _const_spec((D_MODEL, D_MODEL)), _const_spec((1, D_MODEL)),
                  _const_spec((N_EXPERTS, D_MODEL))],
        out_specs=[row4, row4,
                   pl.BlockSpec((1, 2, 1, 8, rows), lambda i: (i // nr, 0, i % nr, 0, 0)),
                   pl.BlockSpec((1, 2, 1, N_EXPERTS, 128), lambda i: (i // nr, 0, i % nr, 0, 0))],
        out_shape=[jax.ShapeDtypeStruct(shape4, F32), jax.ShapeDtypeStruct(shape4, BF16),
                   jax.ShapeDtypeStruct((batch, 2, nr, 8, rows), F32),
                   jax.ShapeDtypeStruct((batch, 2, nr, N_EXPERTS, 128), F32)],
        compiler_params=_cparams(1),
        name="fourier_split_router",
    )(x2d.reshape(shape4), mod, *tables, a.reshape(batch, half_len, 2 * D_MODEL),
      b.reshape(batch, half_len, 2 * D_MODEL), wf, g.reshape(1, D_MODEL), wr_t)
    nb = t // rows
    return (out.reshape(t, D_MODEL), h.reshape(t, D_MODEL), route.reshape(nb, 8, rows),
            cnt.reshape(nb, N_EXPERTS, 128))


def _fourier_router(x2d, a, b, mod, wf, g, wr_t, batch, seq_len, row0):
    t = x2d.shape[0]
    rows = MOE_BLOCK
    if a.shape[1] == 2 * D_MODEL:
        return _fourier_split_router(x2d, a, b, mod, wf, g, wr_t, batch, seq_len, row0)
    part = min(seq_len, rows)
    nbat = rows // part
    nr = seq_len // part
    cl, sl = _dft_tables(seq_len)
    cl = jnp.asarray(cl, dtype=F32).astype(BF16)
    sl = jnp.asarray(-sl, dtype=F32).astype(BF16)
    row_spec = pl.BlockSpec((rows, D_MODEL), lambda i: (i, 0))
    tab_spec = pl.BlockSpec((part, seq_len), lambda i: (i % nr, 0))
    ab_spec = pl.BlockSpec((nbat, seq_len, D_MODEL), lambda i: (i // nr, 0, 0))
    nb = t // rows
    return pl.pallas_call(
        _fourier_router_kernel,
        grid=(nb,),
        in_specs=[row_spec, _mod_spec(nr, row0), tab_spec, tab_spec, ab_spec, ab_spec,
                  _const_spec((D_MODEL, D_MODEL)), _const_spec((1, D_MODEL)),
                  _const_spec((N_EXPERTS, D_MODEL))],
        out_specs=[row_spec, row_spec, pl.BlockSpec((1, 8, rows), lambda i: (i, 0, 0)),
                   pl.BlockSpec((1, N_EXPERTS, 128), lambda i: (i, 0, 0))],
        out_shape=[jax.ShapeDtypeStruct((t, D_MODEL), F32),
                   jax.ShapeDtypeStruct((t, D_MODEL), BF16),
                   jax.ShapeDtypeStruct((nb, 8, rows), F32),
                   jax.ShapeDtypeStruct((nb, N_EXPERTS, 128), F32)],
        compiler_params=_cparams(1),
        name="fourier_router",
    )(x2d, mod, cl, sl, a.reshape(batch, seq_len, D_MODEL), b.reshape(batch, seq_len, D_MODEL),
      wf, g.reshape(1, D_MODEL), wr_t)


def _segment_copies(n16, src_row, dst_row, make_copy):
    @pl.when(n16 > 0)
    def _():
        make_copy(pl.multiple_of(src_row, BF16_SUBLANES), pl.multiple_of(dst_row, BF16_SUBLANES),
                  pl.multiple_of(n16, BF16_SUBLANES))


def _dispatch_kernel(seg_ref, dst_ref, n16_ref, pad_dst_ref, pad_n_ref, nt_ref, *refs,
                     group_blocks, min_tiles):
    n_in = 2 * len(group_blocks)
    xs_ref, comp_ref, zero_ref, sem = refs[n_in:]
    i = pl.program_id(0)
    last = pl.num_programs(0) - 1
    slot = i % 2
    base = i * N_EXPERTS

    def block_copies(base_, slot_, act):
        for e in range(N_EXPERTS):
            _segment_copies(
                n16_ref[base_ + e], seg_ref[base_ + e], dst_ref[base_ + e],
                lambda s, d, n: act(pltpu.make_async_copy(
                    comp_ref.at[slot_, pl.ds(s, n)], xs_ref.at[pl.ds(d, n)], sem.at[slot_])))

    def zero_copies(act):
        for e in range(N_EXPERTS):
            _segment_copies(
                pad_n_ref[e], 0, pad_dst_ref[e],
                lambda s, d, n: act(pltpu.make_async_copy(
                    zero_ref.at[pl.ds(s, n)], xs_ref.at[pl.ds(d, n)], sem.at[2])))
        for tile in range(min_tiles, xs_ref.shape[0] // EXPERT_TILE):
            @pl.when(tile >= nt_ref[0])
            def _():
                act(pltpu.make_async_copy(
                    zero_ref, xs_ref.at[pl.ds(tile * EXPERT_TILE, EXPERT_TILE)], sem.at[2]))

    @pl.when(i == 0)
    def _():
        zero_ref[...] = jnp.zeros_like(zero_ref)
        zero_copies(lambda c: c.start())

    first = 0
    for grp, nb in enumerate(group_blocks):
        h_ref, route_ref = refs[2 * grp], refs[2 * grp + 1]

        @pl.when((i >= first) & (i < first + nb))
        def _():
            r = route_ref[0]
            pos1, pos2 = r[4:5], r[5:6]
            rows = lax.broadcasted_iota(jnp.int32, (MOE_ROWS, MOE_BLOCK), 0).astype(F32)
            onehot = jnp.where((rows == pos1) | (rows == pos2), 1.0, 0.0).astype(BF16)
            comp_ref[slot] = _dot(onehot, h_ref[...]).astype(BF16)
        first += nb
    block_copies(base, slot, lambda c: c.start())

    @pl.when(i > 0)
    def _():
        block_copies(base - N_EXPERTS, 1 - slot, lambda c: c.wait())

    @pl.when(i == last)
    def _():
        block_copies(base, slot, lambda c: c.wait())
        zero_copies(lambda c: c.wait())


def _dispatch(tables, hs, routes, n_rows, min_tiles):
    group_blocks = tuple(h.shape[0] // MOE_BLOCK for h in hs)
    in_specs, args, first = [], [], 0
    for h, route, nb in zip(hs, routes, group_blocks):
        blk = lambda i, first=first, nb=nb: jnp.clip(i - first, 0, nb - 1)
        in_specs += [pl.BlockSpec((MOE_BLOCK, D_MODEL), lambda i, *_, blk=blk: (blk(i), 0)),
                     pl.BlockSpec((1, 8, MOE_BLOCK), lambda i, *_, blk=blk: (blk(i), 0, 0))]
        args += [h, route]
        first += nb
    grid_spec = pltpu.PrefetchScalarGridSpec(
        num_scalar_prefetch=len(tables),
        grid=(sum(group_blocks),),
        in_specs=in_specs,
        out_specs=pl.BlockSpec(memory_space=pl.ANY),
        scratch_shapes=[pltpu.VMEM((2, MOE_ROWS, D_MODEL), BF16),
                        pltpu.VMEM((EXPERT_TILE, D_MODEL), BF16),
                        pltpu.SemaphoreType.DMA((3,))],
    )
    return pl.pallas_call(
        functools.partial(_dispatch_kernel, group_blocks=group_blocks, min_tiles=min_tiles),
        grid_spec=grid_spec,
        out_shape=jax.ShapeDtypeStruct((n_rows, D_MODEL), BF16),
        compiler_params=_cparams(1),
        name="moe_dispatch",
    )(*tables, *args)


def _expert_kernel(te_ref, nt_ref, catch_ref, bge_ref, bgc_ref, xs_ref, wgu_hbm, wd_hbm, ys_ref,
                   wgu_bf, wd_bf, stage_gu, stage_d, sem):
    i = pl.program_id(0)
    used = i < nt_ref[0]
    rg, rd = D_MODEL // W_CHUNKS, D_FF // W_CHUNKS

    def chunk_copies(expert, c, slot):
        return (pltpu.make_async_copy(wgu_hbm.at[expert, pl.ds(pl.multiple_of(c * rg, rg), rg)],
                                      stage_gu.at[slot], sem.at[0, slot]),
                pltpu.make_async_copy(wd_hbm.at[expert, pl.ds(pl.multiple_of(c * rd, rd), rd)],
                                      stage_d.at[slot], sem.at[1, slot]))

    def cast_chunk(expert, c, slot):
        ws = expert % 2
        wgu_bf[ws, pl.ds(pl.multiple_of(c * rg, rg), rg), :] = stage_gu[slot].astype(BF16)
        wd_bf[ws, pl.ds(pl.multiple_of(c * rd, rd), rd), :] = stage_d[slot].astype(BF16)

    @pl.when(used)
    def _():
        e = te_ref[i]

        @pl.when(i == 0)
        def _():
            for cp in chunk_copies(bge_ref[0], bgc_ref[0], 0):
                cp.start()

        @pl.when(i + 1 < nt_ref[0])
        def _():
            for cp in chunk_copies(bge_ref[i + 1], bgc_ref[i + 1], (i + 1) % 2):
                cp.start()

        def catch_up(c, carry):
            cps = chunk_copies(e, c, 2)
            for cp in cps:
                cp.start()
            for cp in cps:
                cp.wait()
            cast_chunk(e, c, 2)
            return carry

        lax.fori_loop(catch_ref[i], W_CHUNKS, catch_up, 0)

        for cp in chunk_copies(bge_ref[i], bgc_ref[i], i % 2):
            cp.wait()
        cast_chunk(bge_ref[i], bgc_ref[i], i % 2)
        ws = e % 2
        ys_ref[...] = _swiglu(xs_ref[...], wgu_bf.at[ws], wd_bf.at[ws]).astype(ys_ref.dtype)

    @pl.when(jnp.logical_not(used))
    def _():
        ys_ref[...] = jnp.zeros_like(ys_ref)


def _experts(tile_tables, xs, wgu_e, wd_e):
    rows = xs.shape[0]
    tm = EXPERT_TILE
    rg, rd = D_MODEL // W_CHUNKS, D_FF // W_CHUNKS
    grid_spec = pltpu.PrefetchScalarGridSpec(
        num_scalar_prefetch=len(tile_tables),
        grid=(rows // tm,),
        in_specs=[pl.BlockSpec((tm, D_MODEL), lambda i, te, nt, *_: (jnp.minimum(i, nt[0] - 1), 0)),
                  pl.BlockSpec(memory_space=pl.ANY), pl.BlockSpec(memory_space=pl.ANY)],
        out_specs=pl.BlockSpec((tm, D_MODEL), lambda i, *_: (i, 0)),
        scratch_shapes=[pltpu.VMEM((2, D_MODEL, 2 * D_FF), BF16), pltpu.VMEM((2, D_FF, D_MODEL), BF16),
                        pltpu.VMEM((3, rg, 2 * D_FF), F32), pltpu.VMEM((3, rd, D_MODEL), F32),
                        pltpu.SemaphoreType.DMA((2, 3))],
    )
    return pl.pallas_call(
        _expert_kernel,
        grid_spec=grid_spec,
        out_shape=jax.ShapeDtypeStruct((rows, D_MODEL), BF16),
        compiler_params=_cparams(1),
        name="moe_experts",
    )(*tile_tables, xs, wgu_e, wd_e)


def _combine_kernel(seg_ref, dst_ref, n16_ref, x_ref, rt_ref, mod_ref, g_ref, ys_ref, out_ref,
                    buf_ref, sem, *, block0):
    i = pl.program_id(0)
    slot = i % 2
    base = (block0 + i) * N_EXPERTS

    def block_copies(base_, slot_, act):
        for e in range(N_EXPERTS):
            _segment_copies(
                n16_ref[base_ + e], seg_ref[base_ + e], dst_ref[base_ + e],
                lambda s, d, n: act(pltpu.make_async_copy(
                    ys_ref.at[pl.ds(d, n)], buf_ref.at[slot_, pl.ds(s, n)], sem.at[slot_])))

    def fetch(base_, slot_):
        buf_ref[slot_] = jnp.zeros(buf_ref.shape[1:], buf_ref.dtype)
        block_copies(base_, slot_, lambda c: c.start())

    @pl.when(i == 0)
    def _():
        fetch(base, slot)

    @pl.when(i + 1 < pl.num_programs(0))
    def _():
        fetch(base + N_EXPERTS, 1 - slot)

    block_copies(base, slot, lambda c: c.wait())
    rt = rt_ref[...]
    pos1, pos2 = rt[:, 4:5], rt[:, 5:6]
    cols = lax.broadcasted_iota(jnp.int32, (MOE_BLOCK, MOE_ROWS), 1).astype(F32)
    gates = (jnp.where(cols == pos1, rt[:, 2:3], 0.0)
             + jnp.where(cols == pos2, rt[:, 3:4], 0.0)).astype(BF16)
    x = x_ref[...] + mod_ref[0, 5:6, :] * _dot(gates, buf_ref[slot])
    out_ref[...] = _rms(x, g_ref[...], EPS)


def _combine(tables, x2d, route_t, mod, g, ys, seq_len, row0, block0):
    t = x2d.shape[0]
    tb = MOE_BLOCK
    grid_spec = pltpu.PrefetchScalarGridSpec(
        num_scalar_prefetch=len(tables),
        grid=(t // tb,),
        in_specs=[pl.BlockSpec((tb, D_MODEL), lambda i, *_: (i, 0)),
                  pl.BlockSpec((tb, 8), lambda i, *_: (i, 0)),
                  _mod_spec(max(seq_len // tb, 1), row0),
                  pl.BlockSpec((1, D_MODEL), lambda i, *_: (0, 0)),
                  pl.BlockSpec(memory_space=pl.ANY)],
        out_specs=pl.BlockSpec((tb, D_MODEL), lambda i, *_: (i, 0)),
        scratch_shapes=[pltpu.VMEM((2, MOE_ROWS, D_MODEL), BF16), pltpu.SemaphoreType.DMA((2,))],
    )
    return pl.pallas_call(
        functools.partial(_combine_kernel, block0=block0),
        grid_spec=grid_spec,
        out_shape=jax.ShapeDtypeStruct((t, D_MODEL), F32),
        compiler_params=_cparams(1),
        name="moe_combine",
    )(*tables, x2d, route_t, mod, g.reshape(1, D_MODEL), ys)


def _moe_tables(counts, n_rows):
    pad = BF16_SUBLANES
    n16 = (counts + pad - 1) // pad * pad
    seg = jnp.cumsum(n16, axis=1) - n16
    total = jnp.sum(n16, axis=0)
    region = (total + EXPERT_TILE - 1) // EXPERT_TILE * EXPERT_TILE
    region_end = jnp.cumsum(region)
    region_start = region_end - region
    dst = region_start[None, :] + jnp.cumsum(n16, axis=0) - n16
    tiles_end = region_end // EXPERT_TILE
    tile_ids = jnp.arange(n_rows // EXPERT_TILE, dtype=jnp.int32)
    tile_expert = jnp.minimum(jnp.sum(tile_ids[:, None] >= tiles_end[None, :], axis=1), N_EXPERTS - 1)
    n_tiles_e = region // EXPERT_TILE
    local = tile_ids - (tiles_end - n_tiles_e)[tile_expert]
    prev_tiles = jnp.concatenate([jnp.zeros((1,), n_tiles_e.dtype), n_tiles_e[:-1]])
    catch_from = jnp.where(local == 0, jnp.minimum(prev_tiles[tile_expert], W_CHUNKS), W_CHUNKS)
    ahead_expert = jnp.where(tile_expert < N_EXPERTS - 1, tile_expert + 1, N_EXPERTS - 2)
    ahead_chunk = jnp.clip(local, 0, W_CHUNKS - 1)
    flat = lambda a: a.reshape(-1).astype(jnp.int32)
    block_tables = (flat(seg), flat(dst), flat(n16))
    pad_tables = (flat(region_start + total), flat(region - total))
    tile_tables = (flat(tile_expert), flat(tiles_end[-1:]), flat(catch_from), flat(ahead_expert),
                   flat(ahead_chunk))
    return block_tables, pad_tables, tile_tables


def kernel(x_prompt, x_sample, c, cache_k_0, cache_v_0, c_ctx, ada_w_0, ada_b_0, norm1_g_0, norm2_g_0, w_qkv_0, lambda_q1_0, lambda_k1_0, lambda_q2_0, lambda_k2_0, subln_g_0, w_o_0, w_gu_0, w_down_0, ada_w_1, ada_b_1, norm1_g_1, norm2_g_1, w_fourier_1, w_router_1, w_gu_e_1, w_down_e_1, final_norm_g):
    bp, lp, _ = x_prompt.shape
    bs, ls, _ = x_sample.shape
    assert 1 + bs <= ADA_ROWS and (bp * lp) % MOE_BLOCK == 0 and ls % MOE_BLOCK == 0
    assert MOE_BLOCK % lp == 0 or lp % MOE_BLOCK == 0

    cond = jnp.zeros((ADA_ROWS, D_MODEL), F32).at[0].set(c_ctx).at[1:1 + bs].set(c)
    mod0, mod1 = _adaln(cond, ada_w_0, ada_b_0, ada_w_1, ada_b_1)
    lam_params = jnp.stack([lambda_q1_0, lambda_k1_0, lambda_q2_0, lambda_k2_0])

    w_qkv = w_qkv_0.astype(BF16)
    w_o = w_o_0.astype(BF16)
    w_gu = w_gu_0.astype(BF16)
    w_down = w_down_0.astype(BF16)
    w_f = w_fourier_1.astype(BF16)
    w_router_t = w_router_1.T

    groups = [dict(x=x_prompt.reshape(bp * lp, D_MODEL), batch=bp, seq=lp, row0=0, rope=False),
              dict(x=x_sample.reshape(bs * ls, D_MODEL), batch=bs, seq=ls, row0=1, rope=True)]

    k_ctx = v_ctx = None
    for gr in groups:
        x, batch, seq, row0 = gr["x"], gr["batch"], gr["seq"], gr["row0"]
        q, k, v = _qkv(x, mod0, norm1_g_0, w_qkv, seq, row0, gr["rope"], BF16 if gr["rope"] else F32)
        if gr["rope"]:
            o = _attention(lam_params, subln_g_0, q, k, v, batch, seq, cache_k_0, cache_v_0,
                           tq=SAMPLE_Q_TILE, heads=SAMPLE_HEADS, sub=SAMPLE_Q_SUB, lag=ATTN_LAG)
        else:
            k_ctx, v_ctx = k, v
            o = _attention(lam_params, subln_g_0, q, k, v, batch, seq, tq=seq, heads=N_HEADS, sub=seq,
                           lag=ATTN_LAG)
        x, fa, fb = _post_attn(x, o, mod0, mod1, norm2_g_0, norm1_g_1, w_o, w_gu, w_down, seq, row0,
                               pair_rows=seq % (2 * MOE_BLOCK) == 0)
        gr["x"], gr["h"], gr["route"], gr["cnt"] = _fourier_router(
            x, fa, fb, mod1, w_f, norm2_g_1, w_router_t, batch, seq, row0)

    n_blocks = [gr["x"].shape[0] // MOE_BLOCK for gr in groups]
    n_pairs = 2 * sum(gr["x"].shape[0] for gr in groups)
    max_rows = n_pairs + sum(n_blocks) * N_EXPERTS * (BF16_SUBLANES - 1) + N_EXPERTS * EXPERT_TILE
    max_rows = (max_rows + EXPERT_TILE - 1) // EXPERT_TILE * EXPERT_TILE
    counts = jnp.concatenate([gr["cnt"][:, :, 0] for gr in groups], axis=0).astype(jnp.int32)
    block_tables, pad_tables, tile_tables = _moe_tables(counts, max_rows)

    xs = _dispatch(block_tables + pad_tables + tile_tables[1:2], [gr["h"] for gr in groups],
                   [gr["route"] for gr in groups], max_rows, n_pairs // EXPERT_TILE)
    ys = _experts(tile_tables, xs, w_gu_e_1, w_down_e_1)
    outs = []
    block0 = 0
    for gr, nb in zip(groups, n_blocks):
        route_t = gr["route"].transpose(0, 2, 1).reshape(-1, 8)
        outs.append(_combine(block_tables, gr["x"], route_t, mod1, final_norm_g, ys,
                             gr["seq"], gr["row0"], block0))
        block0 += nb

    y_prompt = outs[0].reshape(bp, lp, D_MODEL)
    y_sample = outs[1].reshape(bs, ls, D_MODEL)
    return (y_prompt, y_sample,
            k_ctx.reshape(bp, lp, N_HEADS, 2 * HEAD_DIM), v_ctx.reshape(bp, lp, N_HEADS, V_DIM))
```
